```python
import math
import jax, jax.numpy as jnp
from jax import lax
import numpy as np

D_MODEL = 1024
BATCH = 8
SEQ = 4096
DEPTH = 1

CHUNK = 64
HEAD_DIM = 64
MIX_WIDTH = D_MODEL
RWKV_WIDTH = MIX_WIDTH // 2
RET_WIDTH = MIX_WIDTH - RWKV_WIDTH
RWKV_HEADS = RWKV_WIDTH // HEAD_DIM
RET_HEADS = RET_WIDTH // HEAD_DIM
DECAY_LORA = 64
AAA_LORA = 64
GATE_LORA = 128
RWKV_COLS = 3 * RWKV_WIDTH + DECAY_LORA + AAA_LORA + GATE_LORA
RET_COLS = 4 * RET_WIDTH
IN_COLS = RWKV_COLS + RET_COLS
RWKV_SPLITS = (RWKV_WIDTH, 2 * RWKV_WIDTH, 3 * RWKV_WIDTH,
               3 * RWKV_WIDTH + DECAY_LORA, 3 * RWKV_WIDTH + DECAY_LORA + AAA_LORA)
D_FF = 2816
CONV_WIDTH = 3
ROPE_BASE = 10000.0
NORM_EPS = 1e-6
RWKV_LN_EPS = 64e-5
RET_GN_EPS = 1e-6
W_DECAY_SCALE = math.exp(-0.5)
N_MOD = 6

kernel_name = "hymba_rwkv7_retention_convffn_adaln"


def rms_norm(x, g):
    x32 = x.astype(jnp.float32)
    y = x32 * lax.rsqrt(jnp.mean(x32 * x32, axis=-1, keepdims=True) + NORM_EPS)
    return y.astype(x.dtype) * g


def modulate(h, shift, scale):
    return h * (1 + scale[:, None, :]) + shift[:, None, :]


def head_norm(y, eps):
    y32 = y.astype(jnp.float32)
    mu = jnp.mean(y32, axis=-1, keepdims=True)
    var = jnp.mean(jnp.square(y32 - mu), axis=-1, keepdims=True)
    return (y32 - mu) * lax.rsqrt(var + eps)


def rotary(t, cos, sin):
    t1, t2 = jnp.split(t, 2, axis=-1)
    c = cos[None, :, None, :]
    s = sin[None, :, None, :]
    return jnp.concatenate([t1 * c - t2 * s, t1 * s + t2 * c], axis=-1)


def _rwkv7_step(state, inp):
    r, w, k, v, kk, a = inp
    sa = jnp.einsum('bhvk,bhk->bhv', state, -kk)
    state = (state * w[:, :, None, :]
             + sa[..., None] * (kk * a)[:, :, None, :]
             + v[..., None] * k[:, :, None, :])
    y = jnp.einsum('bhvk,bhk->bhv', state, r)
    return state, y


def rwkv7_mixer(z, mu, w0, w2, a0, a2, g2, k_k, k_a, r_k, ln_g, ln_b):
    B, S, _ = z.shape
    z = z.astype(jnp.float32)
    z_prev = jnp.pad(z, ((0, 0), (1, 0), (0, 0)))[:, :S]
    z = z + mu * (z_prev - z)
    r, k, v, wd, ad, gd = jnp.split(z, RWKV_SPLITS, axis=-1)
    decay = jnp.exp(-W_DECAY_SCALE * jax.nn.sigmoid(w0 + jnp.tanh(wd) @ w2))
    a = jax.nn.sigmoid(a0 + ad @ a2)
    g = jax.nn.sigmoid(gd) @ g2
    heads = lambda t: t.reshape(B, S, RWKV_HEADS, HEAD_DIM)
    kk = heads(k * k_k)
    kk = kk / jnp.maximum(jnp.sqrt(jnp.sum(kk * kk, axis=-1, keepdims=True)), 1e-12)
    k = k * (1 + (a - 1) * k_a)
    r, k, v, decay, a = heads(r), heads(k), heads(v), heads(decay), heads(a)
    xs = tuple(jnp.moveaxis(t, 1, 0) for t in (r, decay, k, v, kk, a))
    state0 = jnp.zeros((B, RWKV_HEADS, HEAD_DIM, HEAD_DIM), jnp.float32)
    _, y = lax.scan(_rwkv7_step, state0, xs)
    y = jnp.moveaxis(y, 0, 1)
    y = head_norm(y, RWKV_LN_EPS).reshape(B, S, RWKV_WIDTH) * ln_g + ln_b
    bonus = jnp.sum(r * k * r_k, axis=-1, keepdims=True) * v
    return (y + bonus.reshape(B, S, RWKV_WIDTH)) * g


def retention_mixer(z, cos, sin, gn_g):
    B, S, _ = z.shape
    nc = S // CHUNK
    q, k, v, g = jnp.split(z.astype(jnp.float32), 4, axis=-1)
    heads = lambda t: t.reshape(B, S, RET_HEADS, HEAD_DIM)
    q = rotary(heads(q), cos, sin) * HEAD_DIM ** -0.5
    k = rotary(heads(k), cos, sin)
    v = heads(v)
    chunks = lambda t: t.reshape(B, nc, CHUNK, RET_HEADS, HEAD_DIM).transpose(0, 3, 1, 2, 4)
    q, k, v = chunks(q), chunks(k), chunks(v)
    log_gamma = jnp.log1p(-(2.0 ** (-5.0 - jnp.arange(RET_HEADS, dtype=jnp.float32))))
    idx = jnp.arange(CHUNK, dtype=jnp.float32)
    d_intra = jnp.exp(log_gamma[:, None, None] * jnp.abs(idx[:, None] - idx[None, :]))
    q_dec = jnp.exp(log_gamma[:, None] * (idx + 1.0))
    k_dec = jnp.exp(log_gamma[:, None] * (CHUNK - 1.0 - idx))
    gamma_chunk = jnp.exp(log_gamma * CHUNK)
    scores = jnp.einsum('bhcnd,bhcmd->bhcnm', q, k) * d_intra[None, :, None]
    y = jnp.einsum('bhcnm,bhcme->bhcne', scores, v)
    kv = jnp.einsum('bhcjd,bhcje->cbhde', k * k_dec[None, :, None, :, None], v)

    def advance(state, kv_c):
        return state * gamma_chunk[None, :, None, None] + kv_c, state

    state0 = jnp.zeros((B, RET_HEADS, HEAD_DIM, HEAD_DIM), jnp.float32)
    _, s_prev = lax.scan(advance, state0, kv)
    y = y + jnp.einsum('bhcnd,cbhde->bhcne', q * q_dec[None, :, None, :, None], s_prev)
    y = y.transpose(0, 2, 3, 1, 4).reshape(B, S, RET_HEADS, HEAD_DIM)
    y = head_norm(y, RET_GN_EPS).reshape(B, S, RET_WIDTH) * gn_g
    return jax.nn.silu(g) * y


def conv_glu_ffn(h, w_up, conv_w, conv_b, w_down):
    S = h.shape[1]
    u = h @ w_up
    u_pad = jnp.pad(u, ((0, 0), (CONV_WIDTH - 1, 0), (0, 0)))
    u = conv_b + sum(u_pad[:, j:j + S] * conv_w[j] for j in range(CONV_WIDTH))
    val, gate = jnp.split(u, 2, axis=-1)
    return (jax.nn.silu(gate) * val) @ w_down


def setup_inputs(seed: int = 0) -> dict:
    key = jax.random.key(seed)
    ks = jax.random.split(key, 32)
    f32 = jnp.float32
    nrm = lambda k, shape, s: jax.random.normal(k, shape, f32) * s
    L = DEPTH
    return {
        "x": nrm(ks[0], (BATCH, SEQ, D_MODEL), 1.0),
        "c": nrm(ks[1], (BATCH, D_MODEL), 1.0),
        "w_ada": nrm(ks[2], (L, D_MODEL, N_MOD * D_MODEL), 0.5 * D_MODEL ** -0.5),
        "b_ada": nrm(ks[3], (L, N_MOD * D_MODEL), 0.02),
        "attn_norm_g": 1.0 + nrm(ks[4], (L, D_MODEL), 0.02),
        "w_in": nrm(ks[5], (L, D_MODEL, IN_COLS), D_MODEL ** -0.5),
        "rwkv_mu": jax.random.uniform(ks[6], (L, RWKV_COLS), f32),
        "rwkv_w0": nrm(ks[7], (L, RWKV_WIDTH), 1.5),
        "rwkv_w2": nrm(ks[8], (L, DECAY_LORA, RWKV_WIDTH), DECAY_LORA ** -0.5),
        "rwkv_a0": nrm(ks[9], (L, RWKV_WIDTH), 0.5),
        "rwkv_a2": nrm(ks[10], (L, AAA_LORA, RWKV_WIDTH), 0.5 * AAA_LORA ** -0.5),
        "rwkv_g2": nrm(ks[11], (L, GATE_LORA, RWKV_WIDTH), GATE_LORA ** -0.5),
        "rwkv_k_k": 0.85 + nrm(ks[12], (L, RWKV_WIDTH), 0.05),
        "rwkv_k_a": 1.0 + nrm(ks[13], (L, RWKV_WIDTH), 0.05),
        "rwkv_r_k": nrm(ks[14], (L, RWKV_HEADS, HEAD_DIM), 0.1),
        "rwkv_ln_g": 1.0 + nrm(ks[15], (L, RWKV_WIDTH), 0.02),
        "rwkv_ln_b": nrm(ks[16], (L, RWKV_WIDTH), 0.02),
        "ret_gn_g": 1.0 + nrm(ks[17], (L, RET_WIDTH), 0.02),
        "w_out": nrm(ks[18], (L, MIX_WIDTH, D_MODEL), MIX_WIDTH ** -0.5),
        "ffn_norm_g": 1.0 + nrm(ks[19], (L, D_MODEL), 0.02),
        "ffn_w_up": nrm(ks[20], (L, D_MODEL, 2 * D_FF), D_MODEL ** -0.5),
        "ffn_conv_w": nrm(ks[21], (L, CONV_WIDTH, 2 * D_FF), CONV_WIDTH ** -0.5),
        "ffn_conv_b": nrm(ks[22], (L, 2 * D_FF), 0.02),
        "ffn_w_down": nrm(ks[23], (L, D_FF, D_MODEL), D_FF ** -0.5),
        "final_norm_g": 1.0 + nrm(ks[24], (D_MODEL,), 0.02),
    }


def reference(x, c, w_ada, b_ada, attn_norm_g, w_in, rwkv_mu, rwkv_w0, rwkv_w2, rwkv_a0,
              rwkv_a2, rwkv_g2, rwkv_k_k, rwkv_k_a, rwkv_r_k, rwkv_ln_g, rwkv_ln_b, ret_gn_g,
              w_out, ffn_norm_g, ffn_w_up, ffn_conv_w, ffn_conv_b, ffn_w_down, final_norm_g):
    dt = x.dtype
    S = x.shape[1]
    pos = jnp.arange(S, dtype=jnp.float32)
    inv_freq = ROPE_BASE ** (-jnp.arange(0, HEAD_DIM, 2, dtype=jnp.float32) / HEAD_DIM)
    ang = pos[:, None] * inv_freq[None, :]
    cos, sin = jnp.cos(ang), jnp.sin(ang)
    for l in range(DEPTH):
        mod = jax.nn.silu(c) @ w_ada[l] + b_ada[l]
        sh_a, sc_a, gt_a, sh_f, sc_f, gt_f = jnp.split(mod, N_MOD, axis=-1)
        h = modulate(rms_norm(x, attn_norm_g[l]), sh_a, sc_a)
        z = h @ w_in[l]
        y_rwkv = rwkv7_mixer(z[..., :RWKV_COLS], rwkv_mu[l], rwkv_w0[l], rwkv_w2[l],
                             rwkv_a0[l], rwkv_a2[l], rwkv_g2[l], rwkv_k_k[l], rwkv_k_a[l],
                             rwkv_r_k[l], rwkv_ln_g[l], rwkv_ln_b[l])
        y_ret = retention_mixer(z[..., RWKV_COLS:], cos, sin, ret_gn_g[l])
        y_mix = jnp.concatenate([y_rwkv, y_ret], axis=-1).astype(dt) @ w_out[l]
        x = (x + gt_a[:, None, :] * y_mix).astype(dt)
        h = modulate(rms_norm(x, ffn_norm_g[l]), sh_f, sc_f)
        y_ffn = conv_glu_ffn(h, ffn_w_up[l], ffn_conv_w[l], ffn_conv_b[l], ffn_w_down[l])
        x = (x + gt_f[:, None, :] * y_ffn).astype(dt)
    return rms_norm(x, final_norm_g).astype(dt)
```

```python
import functools
import math

import jax
import jax.numpy as jnp
from jax import lax
from jax.experimental import pallas as pl
from jax.experimental.pallas import tpu as pltpu

F32 = jnp.float32
BF16 = jnp.bfloat16

D_MODEL = 1024
CHUNK = 64
HEAD_DIM = 64
RWKV_WIDTH = 512
RET_WIDTH = 512
N_HEADS = 8
DECAY_LORA = 64
AAA_LORA = 64
GATE_LORA = 128
RWKV_COLS = 3 * RWKV_WIDTH + DECAY_LORA + AAA_LORA + GATE_LORA
RET_COLS = 4 * RET_WIDTH
D_FF = 2816
ROPE_BASE = 10000.0
NORM_EPS = 1e-6
RWKV_LN_EPS = 64e-5
RET_GN_EPS = 1e-6
W_DECAY_SCALE = math.exp(-0.5)
N_MOD = 6

VMEM_LIMIT_BYTES = 56 * 1024 * 1024
INPROJ_ROWS = 256
FFN_ROWS = 512
FFN_COLS = 256
ADA_COLS = 1536

_NT = (((1,), (1,)), ((), ()))


def _bdot(a, b):
    return jnp.dot(a.astype(BF16), b.astype(BF16), preferred_element_type=F32)


def _bdot_nt(a, b):
    return lax.dot_general(a.astype(BF16), b.astype(BF16), _NT, preferred_element_type=F32)


def _split_parts(x, parts=3):
    out, rem = [], x
    for i in range(parts):
        p = rem.astype(BF16)
        out.append(p)
        if i + 1 < parts:
            rem = rem - p.astype(F32)
    return out


def _exact_dot_right(x, m_bf16):
    acc = None
    for p in _split_parts(x):
        d = jnp.dot(p, m_bf16, preferred_element_type=F32)
        acc = d if acc is None else acc + d
    return acc


def _exact_dot_left(m_bf16, x):
    acc = None
    for p in _split_parts(x):
        d = jnp.dot(m_bf16, p, preferred_element_type=F32)
        acc = d if acc is None else acc + d
    return acc


def _sigmoid(x):
    return 1.0 / (1.0 + jnp.exp(-x))


def _rms(x):
    return x * lax.rsqrt(jnp.mean(x * x, axis=-1, keepdims=True) + NORM_EPS)


def _ada_kernel(c_ref, w_ref, b_ref, o_ref):
    cv = c_ref[...]
    s = cv * _sigmoid(cv)
    o_ref[...] = jnp.dot(s, w_ref[...], preferred_element_type=F32,
                         precision=lax.Precision.HIGHEST) + b_ref[...]


def _ada(c, w, b):
    B = c.shape[0]
    n = w.shape[1]
    return pl.pallas_call(
        _ada_kernel,
        grid=(n // ADA_COLS,),
        in_specs=[pl.BlockSpec((B, D_MODEL), lambda j: (0, 0)),
                  pl.BlockSpec((D_MODEL, ADA_COLS), lambda j: (0, j)),
                  pl.BlockSpec((1, ADA_COLS), lambda j: (0, j))],
        out_specs=pl.BlockSpec((B, ADA_COLS), lambda j: (0, j)),
        out_shape=jax.ShapeDtypeStruct((B, n), F32),
        compiler_params=pltpu.CompilerParams(vmem_limit_bytes=VMEM_LIMIT_BYTES),
        name="ada",
    )(c, w, b.reshape(1, n))


def _inproj_kernel(x_ref, g_ref, sc_ref, sh_ref, w_ref, zr_ref, zt_ref):
    h = (_rms(x_ref[0]) * g_ref[...] * (1.0 + sc_ref[0]) + sh_ref[0]).astype(BF16)
    step = 256
    for j in range(0, RWKV_COLS, step):
        zr_ref[0, :, j:j + step] = jnp.dot(h, w_ref[:, j:j + step], preferred_element_type=F32)
    for j in range(0, RET_COLS, step):
        zt_ref[0, :, j:j + step] = jnp.dot(h, w_ref[:, RWKV_COLS + j:RWKV_COLS + j + step],
                                           preferred_element_type=F32)


def _inproj(x, g, sc, sh, w_bf16):
    B, S, D = x.shape
    tm = INPROJ_ROWS
    row = lambda b, t: (b, t, 0)
    per_b = lambda b, t: (b, 0, 0)
    const = lambda b, t: (0, 0)
    return pl.pallas_call(
        _inproj_kernel,
        grid=(B, S // tm),
        in_specs=[pl.BlockSpec((1, tm, D), row),
                  pl.BlockSpec((1, D), const),
                  pl.BlockSpec((1, 1, D), per_b),
                  pl.BlockSpec((1, 1, D), per_b),
                  pl.BlockSpec(w_bf16.shape, const)],
        out_specs=[pl.BlockSpec((1, tm, RWKV_COLS), row),
                   pl.BlockSpec((1, tm, RET_COLS), row)],
        out_shape=[jax.ShapeDtypeStruct((B, S, RWKV_COLS), F32),
                   jax.ShapeDtypeStruct((B, S, RET_COLS), F32)],
        compiler_params=pltpu.CompilerParams(
            dimension_semantics=("parallel", "arbitrary"), vmem_limit_bytes=VMEM_LIMIT_BYTES),
        name="inproj",
    )(x, g.reshape(1, D), sc, sh, w_bf16)


def _head_norm(y, seg, eps):
    mean = _exact_dot_right(y, seg) * (1.0 / HEAD_DIM)
    d = y - mean
    var = _exact_dot_right(d * d, seg) * (1.0 / HEAD_DIM)
    return d * lax.rsqrt(var + eps)


def _rwkv_kernel(z_ref, mu_ref, w0_ref, w2_ref, a0_ref, a2_ref, g2_ref, kk_ref, ka_ref, rk_ref,
                 lng_ref, lnb_ref, seg_ref, y_ref, zlast_ref, state_ref, ybuf_ref):
    C = CHUNK
    W = RWKV_WIDTH

    @pl.when(pl.program_id(1) == 0)
    def _():
        zlast_ref[...] = jnp.zeros_like(zlast_ref)
        state_ref[...] = jnp.zeros_like(state_ref)

    z = z_ref[0]
    row = lax.broadcasted_iota(jnp.int32, (C, 1), 0)
    zprev = jnp.where(row == 0, zlast_ref[0:1, :], pltpu.roll(z, 1, 0))
    zlast_ref[0:1, :] = z[C - 1:C, :]
    zm = z + mu_ref[...] * (zprev - z)
    r = zm[:, 0:W]
    k = zm[:, W:2 * W]
    v = zm[:, 2 * W:3 * W]
    wd = zm[:, 3 * W:3 * W + DECAY_LORA]
    ad = zm[:, 3 * W + DECAY_LORA:3 * W + DECAY_LORA + AAA_LORA]
    gd = zm[:, 3 * W + DECAY_LORA + AAA_LORA:RWKV_COLS]

    lw = -W_DECAY_SCALE * _sigmoid(w0_ref[...] + _bdot(jnp.tanh(wd), w2_ref[...]))
    a = _sigmoid(a0_ref[...] + _bdot(ad, a2_ref[...]))
    g = _bdot(_sigmoid(gd), g2_ref[...])

    seg = seg_ref[...]
    kkr = k * kk_ref[...]
    nrm = jnp.sqrt(_exact_dot_right(kkr * kkr, seg))
    kkn = kkr / jnp.maximum(nrm, 1e-12)
    km = k * (1.0 + (a - 1.0) * ka_ref[...])

    ri = lax.broadcasted_iota(jnp.int32, (C, C), 0)
    ci = lax.broadcasted_iota(jnp.int32, (C, C), 1)
    tri = (ri >= ci).astype(BF16)
    lp = _exact_dot_left(tri, lw)
    lpc = lp[C - 1:C, :]
    rt = r * jnp.exp(lp)
    at = -kkn * jnp.exp(lp - lw)
    em = jnp.exp(-lp)
    kb = kkn * a
    bt = kb * em
    kt = km * em
    ee = jnp.exp(lpc - lp)
    gt = jnp.concatenate([kb * ee, km * ee], axis=0).T
    pcol = jnp.exp(jnp.concatenate([lp, lp], axis=0).T[:, C - 1:C])

    strict = ri > ci
    ri2 = lax.broadcasted_iota(jnp.int32, (C, 2 * C), 0)
    ci2 = lax.broadcasted_iota(jnp.int32, (C, 2 * C), 1)
    incl2 = ri2 >= jnp.where(ci2 >= C, ci2 - C, ci2)
    eye = (ri == ci).astype(F32)

    for h in range(N_HEADS):
        lo, hi = h * HEAD_DIM, (h + 1) * HEAD_DIM
        at_h, rt_h, v_h = at[:, lo:hi], rt[:, lo:hi], v[:, lo:hi]
        rhs = jnp.concatenate([bt[:, lo:hi], kt[:, lo:hi]], axis=0)
        la = _bdot_nt(at_h, rhs)
        mc = jnp.where(incl2, _bdot_nt(rt_h, rhs), 0.0)
        l_ab = jnp.where(strict, la[:, 0:C], 0.0)
        l_ak = jnp.where(strict, la[:, C:2 * C], 0.0)
        tinv = eye + l_ab
        pw = l_ab
        for _ in range(5):
            pw = _bdot(pw, pw)
            tinv = tinv + _bdot(tinv, pw)
        h0 = state_ref[h]
        ah = _bdot(jnp.concatenate([at_h, rt_h], axis=0), h0)
        u = _bdot(tinv, ah[0:C] + _bdot(l_ak, v_h))
        uv = jnp.concatenate([u, v_h], axis=0)
        ybuf_ref[:, lo:hi] = ah[C:2 * C] + _bdot(mc, uv)
        state_ref[h] = pcol[lo:hi, :] * h0 + _bdot(gt[lo:hi, :], uv)

    y = _head_norm(ybuf_ref[...], seg, RWKV_LN_EPS) * lng_ref[...] + lnb_ref[...]
    bonus = _exact_dot_right(r * km * rk_ref[...], seg) * v
    y_ref[0] = (y + bonus) * g


def _rwkv(z, mu, w0, w2, a0, a2, g2, k_k, k_a, r_k, ln_g, ln_b, seg):
    B, S, _ = z.shape
    W = RWKV_WIDTH
    row = lambda b, t: (b, t, 0)
    const = lambda b, t: (0, 0)
    vec = lambda a: a.reshape(1, -1)
    full = lambda a: pl.BlockSpec(a.shape, const)
    args = [vec(mu), vec(w0), w2.astype(BF16), vec(a0), a2.astype(BF16), g2.astype(BF16),
            vec(k_k), vec(k_a), vec(r_k), vec(ln_g), vec(ln_b), seg]
    return pl.pallas_call(
        _rwkv_kernel,
        grid=(B, S // CHUNK),
        in_specs=[pl.BlockSpec((1, CHUNK, RWKV_COLS), row)] + [full(a) for a in args],
        out_specs=pl.BlockSpec((1, CHUNK, W), row),
        out_shape=jax.ShapeDtypeStruct((B, S, W), F32),
        scratch_shapes=[pltpu.VMEM((8, RWKV_COLS), F32),
                        pltpu.VMEM((N_HEADS, HEAD_DIM, HEAD_DIM), F32),
                        pltpu.VMEM((CHUNK, W), F32)],
        compiler_params=pltpu.CompilerParams(
            dimension_semantics=("parallel", "arbitrary"), vmem_limit_bytes=VMEM_LIMIT_BYTES),
        name="rwkv",
    )(z, *args)


def _ret_kernel(z_ref, cos_ref, sin_ref, dm_ref, qdec_ref, kdec_ref, gc_ref, gng_ref, seg_ref,
                y_ref, state_ref, ybuf_ref):
    C = CHUNK
    W = RET_WIDTH

    @pl.when(pl.program_id(1) == 0)
    def _():
        state_ref[...] = jnp.zeros_like(state_ref)

    z = z_ref[0]
    q = z[:, 0:W]
    k = z[:, W:2 * W]
    v = z[:, 2 * W:3 * W]
    g = z[:, 3 * W:4 * W]
    reps = W // cos_ref.shape[1]
    cosf = jnp.concatenate([cos_ref[...]] * reps, axis=1)
    sinf = jnp.concatenate([sin_ref[...]] * reps, axis=1)
    lane = lax.broadcasted_iota(jnp.int32, (1, W), 1)
    first_half = (lane % HEAD_DIM) < (HEAD_DIM // 2)

    def rot(t):
        swapped = jnp.where(first_half, pltpu.roll(t, W - HEAD_DIM // 2, 1), pltpu.roll(t, HEAD_DIM // 2, 1))
        return t * cosf + swapped * sinf

    q = rot(q) * (HEAD_DIM ** -0.5)
    k = rot(k)
    qd = q * qdec_ref[...]
    kdt = jnp.concatenate([k * kdec_ref[...]] * 2, axis=0).T

    for h in range(N_HEADS):
        lo, hi = h * HEAD_DIM, (h + 1) * HEAD_DIM
        v_h = v[:, lo:hi]
        sc = _bdot_nt(q[:, lo:hi], k[:, lo:hi]) * dm_ref[h]
        s0 = state_ref[h]
        ybuf_ref[:, lo:hi] = _bdot(sc, v_h) + _bdot(qd[:, lo:hi], s0)
        state_ref[h] = s0 * gc_ref[h] + _bdot(kdt[lo:hi, 0:C], v_h)

    y = _head_norm(ybuf_ref[...], seg_ref[...], RET_GN_EPS) * gng_ref[...]
    y_ref[0] = g * _sigmoid(g) * y


def _ret(z, cos2, sin2, dmat, qdec, kdec, gc, gn_g, seg):
    B, S, _ = z.shape
    W = RET_WIDTH
    row = lambda b, t: (b, t, 0)
    args = [dmat, qdec, kdec, gc, gn_g.reshape(1, W), seg]
    full = lambda a: pl.BlockSpec(a.shape, lambda b, t: (0,) * a.ndim)
    return pl.pallas_call(
        _ret_kernel,
        grid=(B, S // CHUNK),
        in_specs=[pl.BlockSpec((1, CHUNK, RET_COLS), row),
                  pl.BlockSpec((CHUNK, cos2.shape[1]), lambda b, t: (t, 0)),
                  pl.BlockSpec((CHUNK, sin2.shape[1]), lambda b, t: (t, 0))] + [full(a) for a in args],
        out_specs=pl.BlockSpec((1, CHUNK, W), row),
        out_shape=jax.ShapeDtypeStruct((B, S, W), F32),
        scratch_shapes=[pltpu.VMEM((N_HEADS, HEAD_DIM, HEAD_DIM), F32),
                        pltpu.VMEM((CHUNK, W), F32)],
        compiler_params=pltpu.CompilerParams(
            dimension_semantics=("parallel", "arbitrary"), vmem_limit_bytes=VMEM_LIMIT_BYTES),
        name="ret",
    )(z, cos2, sin2, *args)


def _ffn_kernel(x_ref, yr_ref, yt_ref, gta_ref, shf_ref, scf_ref, gtf_ref, wout_ref, gnf_ref,
                wup_ref, cw_ref, cb_ref, wdn_ref, gfin_ref, o_ref, carry_ref, acc_ref):
    tm = x_ref.shape[1]

    @pl.when(pl.program_id(1) == 0)
    def _():
        carry_ref[...] = jnp.zeros_like(carry_ref)

    ymix = (jnp.dot(yr_ref[0].astype(BF16), wout_ref[0:RWKV_WIDTH, :], preferred_element_type=F32)
            + jnp.dot(yt_ref[0].astype(BF16), wout_ref[RWKV_WIDTH:, :], preferred_element_type=F32))
    x1 = x_ref[0] + gta_ref[0] * ymix
    h = (_rms(x1) * gnf_ref[...] * (1.0 + scf_ref[0]) + shf_ref[0]).astype(BF16)
    rowi = lax.broadcasted_iota(jnp.int32, (tm, 1), 0)

    def conv(u, col):
        tf = u.shape[1]
        prev = carry_ref[:, col:col + tf]
        u1 = jnp.where(rowi == 0, prev[7:8, :], pltpu.roll(u, 1, 0))
        u2 = jnp.where(rowi == 0, prev[6:7, :], jnp.where(rowi == 1, prev[7:8, :], pltpu.roll(u, 2, 0)))
        carry_ref[:, col:col + tf] = u[tm - 8:tm, :]
        cw = cw_ref[:, col:col + tf]
        return cb_ref[:, col:col + tf] + u2 * cw[0:1, :] + u1 * cw[1:2, :] + u * cw[2:3, :]

    tf = FFN_COLS
    for n, j in enumerate(range(0, D_FF, tf)):
        val = conv(jnp.dot(h, wup_ref[:, j:j + tf], preferred_element_type=F32), j)
        gate = conv(jnp.dot(h, wup_ref[:, D_FF + j:D_FF + j + tf], preferred_element_type=F32), D_FF + j)
        act = (gate * _sigmoid(gate) * val).astype(BF16)
        part = jnp.dot(act, wdn_ref[j:j + tf, :], preferred_element_type=F32)
        if n == 0:
            acc_ref[...] = part
        else:
            acc_ref[...] += part
    x2 = x1 + gtf_ref[0] * acc_ref[...]
    o_ref[0] = _rms(x2) * gfin_ref[...]


def _ffn(x, y_rwkv, y_ret, gt_a, sh_f, sc_f, gt_f, w_out, g_ffn, w_up, conv_w, conv_b, w_down, g_fin):
    B, S, D = x.shape
    tm = FFN_ROWS
    row = lambda b, t: (b, t, 0)
    per_b = lambda b, t: (b, 0, 0)
    const = lambda b, t: (0, 0)
    once = lambda a: pl.BlockSpec(a.shape, const, pipeline_mode=pl.Buffered(1))
    vecs = [g_ffn.reshape(1, D)]
    return pl.pallas_call(
        _ffn_kernel,
        grid=(B, S // tm),
        in_specs=[pl.BlockSpec((1, tm, D), row),
                  pl.BlockSpec((1, tm, RWKV_WIDTH), row),
                  pl.BlockSpec((1, tm, RET_WIDTH), row),
                  pl.BlockSpec((1, 1, D), per_b), pl.BlockSpec((1, 1, D), per_b),
                  pl.BlockSpec((1, 1, D), per_b), pl.BlockSpec((1, 1, D), per_b),
                  once(w_out), once(vecs[0]), once(w_up), once(conv_w), pl.BlockSpec((1, 2 * D_FF), const),
                  once(w_down), pl.BlockSpec((1, D), const)],
        out_specs=pl.BlockSpec((1, tm, D), row),
        out_shape=jax.ShapeDtypeStruct((B, S, D), F32),
        scratch_shapes=[pltpu.VMEM((8, 2 * D_FF), F32),
                        pltpu.VMEM((tm, D), F32)],
        compiler_params=pltpu.CompilerParams(
            dimension_semantics=("parallel", "arbitrary"), vmem_limit_bytes=VMEM_LIMIT_BYTES),
        name="ffn",
    )(x, y_rwkv, y_ret, gt_a, sh_f, sc_f, gt_f, w_out, vecs[0], w_up, conv_w,
      conv_b.reshape(1, 2 * D_FF), w_down, g_fin.reshape(1, D))


def _tables(S):
    half = HEAD_DIM // 2
    pos = jnp.arange(S, dtype=F32)
    inv_freq = ROPE_BASE ** (-jnp.arange(0, HEAD_DIM, 2, dtype=F32) / HEAD_DIM)
    ang = pos[:, None] * inv_freq[None, :]
    cos, sin = jnp.cos(ang), jnp.sin(ang)
    cos2 = jnp.concatenate([cos, cos, cos, cos], axis=1)
    sin2 = jnp.concatenate([-sin, sin, -sin, sin], axis=1)
    log_gamma = jnp.log1p(-(2.0 ** (-5.0 - jnp.arange(N_HEADS, dtype=F32))))
    idx = jnp.arange(CHUNK, dtype=F32)
    dmat = jnp.exp(log_gamma[:, None, None] * jnp.abs(idx[:, None] - idx[None, :]))
    q_dec = jnp.exp(log_gamma[:, None] * (idx + 1.0))
    k_dec = jnp.exp(log_gamma[:, None] * (CHUNK - 1.0 - idx))
    widen = lambda t: jnp.repeat(t.T, HEAD_DIM, axis=1)
    gamma_chunk = jnp.exp(log_gamma * CHUNK)
    gc = jnp.broadcast_to(gamma_chunk[:, None, None], (N_HEADS, 1, HEAD_DIM))
    head = jnp.arange(RWKV_WIDTH) // HEAD_DIM
    seg = (head[:, None] == head[None, :]).astype(BF16)
    return cos2, sin2, dmat, widen(q_dec), widen(k_dec), gc, seg


def kernel(x, c, w_ada, b_ada, attn_norm_g, w_in, rwkv_mu, rwkv_w0, rwkv_w2, rwkv_a0, rwkv_a2, rwkv_g2,
           rwkv_k_k, rwkv_k_a, rwkv_r_k, rwkv_ln_g, rwkv_ln_b, ret_gn_g, w_out, ffn_norm_g, ffn_w_up,
           ffn_conv_w, ffn_conv_b, ffn_w_down, final_norm_g):
    B, S, D = x.shape
    assert D == D_MODEL and S % FFN_ROWS == 0 and w_ada.shape[0] == 1
    cos2, sin2, dmat, qdec, kdec, gc, seg = _tables(S)
    mod = _ada(c, w_ada[0], b_ada[0])
    sh_a, sc_a, gt_a, sh_f, sc_f, gt_f = [m.reshape(B, 1, D) for m in jnp.split(mod, N_MOD, axis=-1)]
    z_rwkv, z_ret = _inproj(x, attn_norm_g[0], sc_a, sh_a, w_in[0].astype(BF16))
    y_rwkv = _rwkv(z_rwkv, rwkv_mu[0], rwkv_w0[0], rwkv_w2[0], rwkv_a0[0], rwkv_a2[0], rwkv_g2[0],
                   rwkv_k_k[0], rwkv_k_a[0], rwkv_r_k[0], rwkv_ln_g[0], rwkv_ln_b[0], seg)
    y_ret = _ret(z_ret, cos2, sin2, dmat, qdec, kdec, gc, ret_gn_g[0], seg)
    return _ffn(x, y_rwkv, y_ret, gt_a, sh_f, sc_f, gt_f, w_out[0].astype(BF16), ffn_norm_g[0],
                ffn_w_up[0].astype(BF16), ffn_conv_w[0], ffn_conv_b[0], ffn_w_down[0].astype(BF16),
                final_norm_g)
```

```python
import functools
import math

import jax
import jax.numpy as jnp
from jax import lax
from jax.experimental import pallas as pl
from jax.experimental.pallas import tpu as pltpu

F32 = jnp.float32
BF16 = jnp.bfloat16

D_MODEL = 1024
CHUNK = 64
HEAD_DIM = 64
RWKV_WIDTH = 512
RET_WIDTH = 512
N_HEADS = 8
DECAY_LORA = 64
AAA_LORA = 64
GATE_LORA = 128
RWKV_COLS = 3 * RWKV_WIDTH + DECAY_LORA + AAA_LORA + GATE_LORA
RET_COLS = 4 * RET_WIDTH
D_FF = 2816
ROPE_BASE = 10000.0
NORM_EPS = 1e-6
RWKV_LN_EPS = 64e-5
RET_GN_EPS = 1e-6
W_DECAY_SCALE = math.exp(-0.5)
N_MOD = 6

VMEM_LIMIT_BYTES = 56 * 1024 * 1024
INPROJ_ROWS = 256
RWKV_TOKENS = 256
FFN_ROWS = 512
FFN_COLS = 256
ADA_COLS = 1536

_NT = (((1,), (1,)), ((), ()))


def _bdot(a, b):
    return jnp.dot(a.astype(BF16), b.astype(BF16), preferred_element_type=F32)


def _bdot_nt(a, b):
    return lax.dot_general(a.astype(BF16), b.astype(BF16), _NT, preferred_element_type=F32)


def _split_parts(x, parts=3):
    out, rem = [], x
    for i in range(parts):
        p = rem.astype(BF16)
        out.append(p)
        if i + 1 < parts:
            rem = rem - p.astype(F32)
    return out


def _exact_dot_right(x, m_bf16):
    acc = None
    for p in _split_parts(x):
        d = jnp.dot(p, m_bf16, preferred_element_type=F32)
        acc = d if acc is None else acc + d
    return acc


def _exact_dot_left(m_bf16, x):
    acc = None
    for p in _split_parts(x):
        d = jnp.dot(m_bf16, p, preferred_element_type=F32)
        acc = d if acc is None else acc + d
    return acc


def _sigmoid(x):
    return 1.0 / (1.0 + jnp.exp(-x))


def _rms(x):
    return x * lax.rsqrt(jnp.mean(x * x, axis=-1, keepdims=True) + NORM_EPS)


def _ada_kernel(c_ref, w_ref, b_ref, o_ref):
    cv = c_ref[...]
    s = cv * _sigmoid(cv)
    o_ref[...] = jnp.dot(s, w_ref[...], preferred_element_type=F32,
                         precision=lax.Precision.HIGHEST) + b_ref[...]


def _ada(c, w, b):
    B = c.shape[0]
    n = w.shape[1]
    return pl.pallas_call(
        _ada_kernel,
        grid=(n // ADA_COLS,),
        in_specs=[pl.BlockSpec((B, D_MODEL), lambda j: (0, 0)),
                  pl.BlockSpec((D_MODEL, ADA_COLS), lambda j: (0, j)),
                  pl.BlockSpec((1, ADA_COLS), lambda j: (0, j))],
        out_specs=pl.BlockSpec((B, ADA_COLS), lambda j: (0, j)),
        out_shape=jax.ShapeDtypeStruct((B, n), F32),
        compiler_params=pltpu.CompilerParams(vmem_limit_bytes=VMEM_LIMIT_BYTES),
        name="ada",
    )(c, w, b.reshape(1, n))


def _inproj_kernel(x_ref, g_ref, sc_ref, sh_ref, w_ref, zr_ref, zt_ref):
    h = (_rms(x_ref[0]) * g_ref[...] * (1.0 + sc_ref[0]) + sh_ref[0]).astype(BF16)
    step = 256
    for j in range(0, RWKV_COLS, step):
        zr_ref[0, :, j:j + step] = jnp.dot(h, w_ref[:, j:j + step], preferred_element_type=F32)
    for j in range(0, RET_COLS, step):
        zt_ref[0, :, j:j + step] = jnp.dot(h, w_ref[:, RWKV_COLS + j:RWKV_COLS + j + step],
                                           preferred_element_type=F32)


def _inproj(x, g, sc, sh, w_bf16):
    B, S, D = x.shape
    tm = INPROJ_ROWS
    row = lambda b, t: (b, t, 0)
    per_b = lambda b, t: (b, 0, 0)
    const = lambda b, t: (0, 0)
    return pl.pallas_call(
        _inproj_kernel,
        grid=(B, S // tm),
        in_specs=[pl.BlockSpec((1, tm, D), row),
                  pl.BlockSpec((1, D), const),
                  pl.BlockSpec((1, 1, D), per_b),
                  pl.BlockSpec((1, 1, D), per_b),
                  pl.BlockSpec(w_bf16.shape, const)],
        out_specs=[pl.BlockSpec((1, tm, RWKV_COLS), row),
                   pl.BlockSpec((1, tm, RET_COLS), row)],
        out_shape=[jax.ShapeDtypeStruct((B, S, RWKV_COLS), F32),
                   jax.ShapeDtypeStruct((B, S, RET_COLS), F32)],
        compiler_params=pltpu.CompilerParams(
            dimension_semantics=("parallel", "arbitrary"), vmem_limit_bytes=VMEM_LIMIT_BYTES),
        name="inproj",
    )(x, g.reshape(1, D), sc, sh, w_bf16)


def _head_norm(y, seg, eps):
    mean = _exact_dot_right(y, seg) * (1.0 / HEAD_DIM)
    d = y - mean
    var = _exact_dot_right(d * d, seg) * (1.0 / HEAD_DIM)
    return d * lax.rsqrt(var + eps)


def _rwkv_kernel(z_ref, mu_ref, w0_ref, w2_ref, a0_ref, a2_ref, g2_ref, kk_ref, ka_ref, rk_ref,
                 lng_ref, lnb_ref, seg_ref, y_ref, zlast_ref, state_ref, ybuf_ref):
    C = CHUNK
    W = RWKV_WIDTH
    T = z_ref.shape[1]
    n_chunks = T // C

    @pl.when(pl.program_id(1) == 0)
    def _():
        zlast_ref[...] = jnp.zeros_like(zlast_ref)
        state_ref[...] = jnp.zeros_like(state_ref)

    z = z_ref[0]
    row = lax.broadcasted_iota(jnp.int32, (T, 1), 0)
    zprev = jnp.where(row == 0, zlast_ref[0:1, :], pltpu.roll(z, 1, 0))
    zlast_ref[0:1, :] = z[T - 1:T, :]
    zm = z + mu_ref[...] * (zprev - z)
    r = zm[:, 0:W]
    k = zm[:, W:2 * W]
    v = zm[:, 2 * W:3 * W]
    wd = zm[:, 3 * W:3 * W + DECAY_LORA]
    ad = zm[:, 3 * W + DECAY_LORA:3 * W + DECAY_LORA + AAA_LORA]
    gd = zm[:, 3 * W + DECAY_LORA + AAA_LORA:RWKV_COLS]

    lw = -W_DECAY_SCALE * _sigmoid(w0_ref[...] + _bdot(jnp.tanh(wd), w2_ref[...]))
    a = _sigmoid(a0_ref[...] + _bdot(ad, a2_ref[...]))
    g = _bdot(_sigmoid(gd), g2_ref[...])

    seg = seg_ref[...]
    kkr = k * kk_ref[...]
    nrm = jnp.sqrt(_exact_dot_right(kkr * kkr, seg))
    kkn = kkr / jnp.maximum(nrm, 1e-12)
    km = k * (1.0 + (a - 1.0) * ka_ref[...])
    kb = kkn * a

    ri = lax.broadcasted_iota(jnp.int32, (C, C), 0)
    ci = lax.broadcasted_iota(jnp.int32, (C, C), 1)
    tri = (ri >= ci).astype(BF16)
    strict = ri > ci
    ri2 = lax.broadcasted_iota(jnp.int32, (C, 2 * C), 0)
    ci2 = lax.broadcasted_iota(jnp.int32, (C, 2 * C), 1)
    incl2 = ri2 >= jnp.where(ci2 >= C, ci2 - C, ci2)
    eye = (ri == ci).astype(F32)

    units = []
    for c in range(n_chunks):
        rows = slice(c * C, (c + 1) * C)
        lw_c = lw[rows]
        lp = _exact_dot_left(tri, lw_c)
        rt = r[rows] * jnp.exp(lp)
        at = -kkn[rows] * jnp.exp(lp - lw_c)
        em = jnp.exp(-lp)
        bt = kb[rows] * em
        kt = km[rows] * em
        ee = jnp.exp(lp[C - 1:C, :] - lp)
        gt = jnp.concatenate([kb[rows] * ee, km[rows] * ee], axis=0).T
        pcol = jnp.exp(jnp.concatenate([lp, lp], axis=0).T[:, C - 1:C])
        for h in range(N_HEADS):
            lo, hi = h * HEAD_DIM, (h + 1) * HEAD_DIM
            units.append(dict(c=c, h=h, at=at[:, lo:hi], rt=rt[:, lo:hi], v=v[rows, lo:hi],
                              rhs=jnp.concatenate([bt[:, lo:hi], kt[:, lo:hi]], axis=0),
                              gt=gt[lo:hi, :], pcol=pcol[lo:hi, :]))
    for u in units:
        la = _bdot_nt(u["at"], u["rhs"])
        u["mc"] = jnp.where(incl2, _bdot_nt(u["rt"], u["rhs"]), 0.0)
        u["pw"] = jnp.where(strict, la[:, 0:C], 0.0)
        u["l_ak"] = jnp.where(strict, la[:, C:2 * C], 0.0)
        u["tinv"] = eye + u["pw"]
    for _ in range(5):
        for u in units:
            u["pw"] = _bdot(u["pw"], u["pw"])
        for u in units:
            u["tinv"] = u["tinv"] + _bdot(u["tinv"], u["pw"])
    for u in units:
        u["x"] = _bdot(u["l_ak"], u["v"])
    for u in units:
        u["w"] = _bdot(u["tinv"], u["at"])
        u["uv0"] = jnp.concatenate([_bdot(u["tinv"], u["x"]), u["v"]], axis=0)
    for u in units:
        phi = _bdot(u["gt"][:, 0:C], u["w"])
        om = u["rt"] + _bdot(u["mc"][:, 0:C], u["w"])
        u["po"] = jnp.concatenate([phi, om], axis=0)
        u["y0"] = _bdot(u["mc"], u["uv0"])
        u["psi"] = _bdot(u["gt"], u["uv0"])

    hcur = [state_ref[h] for h in range(N_HEADS)]
    for u in units:
        c, h = u["c"], u["h"]
        ph = _bdot(u["po"], hcur[h])
        ybuf_ref[c * C:(c + 1) * C, h * HEAD_DIM:(h + 1) * HEAD_DIM] = ph[C:2 * C] + u["y0"]
        hcur[h] = u["pcol"] * hcur[h] + ph[0:C] + u["psi"]
    for h in range(N_HEADS):
        state_ref[h] = hcur[h]

    y = _head_norm(ybuf_ref[...], seg, RWKV_LN_EPS) * lng_ref[...] + lnb_ref[...]
    bonus = _exact_dot_right(r * km * rk_ref[...], seg) * v
    y_ref[0] = (y + bonus) * g


def _rwkv(z, mu, w0, w2, a0, a2, g2, k_k, k_a, r_k, ln_g, ln_b, seg):
    B, S, _ = z.shape
    W = RWKV_WIDTH
    T = RWKV_TOKENS
    row = lambda b, t: (b, t, 0)
    const = lambda b, t: (0, 0)
    vec = lambda a: a.reshape(1, -1)
    full = lambda a: pl.BlockSpec(a.shape, const)
    args = [vec(mu), vec(w0), w2.astype(BF16), vec(a0), a2.astype(BF16), g2.astype(BF16),
            vec(k_k), vec(k_a), vec(r_k), vec(ln_g), vec(ln_b), seg]
    return pl.pallas_call(
        _rwkv_kernel,
        grid=(B, S // T),
        in_specs=[pl.BlockSpec((1, T, RWKV_COLS), row)] + [full(a) for a in args],
        out_specs=pl.BlockSpec((1, T, W), row),
        out_shape=jax.ShapeDtypeStruct((B, S, W), F32),
        scratch_shapes=[pltpu.VMEM((8, RWKV_COLS), F32),
                        pltpu.VMEM((N_HEADS, HEAD_DIM, HEAD_DIM), F32),
                        pltpu.VMEM((T, W), F32)],
        compiler_params=pltpu.CompilerParams(
            dimension_semantics=("parallel", "arbitrary"), vmem_limit_bytes=VMEM_LIMIT_BYTES),
        name="rwkv",
    )(z, *args)


def _ret_kernel(z_ref, cos_ref, sin_ref, dm_ref, qdec_ref, kdec_ref, gc_ref, gng_ref, seg_ref,
                y_ref, state_ref, ybuf_ref):
    C = CHUNK
    W = RET_WIDTH

    @pl.when(pl.program_id(1) == 0)
    def _():
        state_ref[...] = jnp.zeros_like(state_ref)

    z = z_ref[0]
    q = z[:, 0:W]
    k = z[:, W:2 * W]
    v = z[:, 2 * W:3 * W]
    g = z[:, 3 * W:4 * W]
    reps = W // cos_ref.shape[1]
    cosf = jnp.concatenate([cos_ref[...]] * reps, axis=1)
    sinf = jnp.concatenate([sin_ref[...]] * reps, axis=1)
    lane = lax.broadcasted_iota(jnp.int32, (1, W), 1)
    first_half = (lane % HEAD_DIM) < (HEAD_DIM // 2)

    def rot(t):
        swapped = jnp.where(first_half, pltpu.roll(t, W - HEAD_DIM // 2, 1), pltpu.roll(t, HEAD_DIM // 2, 1))
        return t * cosf + swapped * sinf

    q = rot(q) * (HEAD_DIM ** -0.5)
    k = rot(k)
    qd = q * qdec_ref[...]
    kdt = jnp.concatenate([k * kdec_ref[...]] * 2, axis=0).T

    for h in range(N_HEADS):
        lo, hi = h * HEAD_DIM, (h + 1) * HEAD_DIM
        v_h = v[:, lo:hi]
        sc = _bdot_nt(q[:, lo:hi], k[:, lo:hi]) * dm_ref[h]
        s0 = state_ref[h]
        ybuf_ref[:, lo:hi] = _bdot(sc, v_h) + _bdot(qd[:, lo:hi], s0)
        state_ref[h] = s0 * gc_ref[h] + _bdot(kdt[lo:hi, 0:C], v_h)

    y = _head_norm(ybuf_ref[...], seg_ref[...], RET_GN_EPS) * gng_ref[...]
    y_ref[0] = g * _sigmoid(g) * y


def _ret(z, cos2, sin2, dmat, qdec, kdec, gc, gn_g, seg):
    B, S, _ = z.shape
    W = RET_WIDTH
    row = lambda b, t: (b, t, 0)
    args = [dmat, qdec, kdec, gc, gn_g.reshape(1, W), seg]
    full = lambda a: pl.BlockSpec(a.shape, lambda b, t: (0,) * a.ndim)
    return pl.pallas_call(
        _ret_kernel,
        grid=(B, S // CHUNK),
        in_specs=[pl.BlockSpec((1, CHUNK, RET_COLS), row),
                  pl.BlockSpec((CHUNK, cos2.shape[1]), lambda b, t: (t, 0)),
                  pl.BlockSpec((CHUNK, sin2.shape[1]), lambda b, t: (t, 0))] + [full(a) for a in args],
        out_specs=pl.BlockSpec((1, CHUNK, W), row),
        out_shape=jax.ShapeDtypeStruct((B, S, W), F32),
        scratch_shapes=[pltpu.VMEM((N_HEADS, HEAD_DIM, HEAD_DIM), F32),
                        pltpu.VMEM((CHUNK, W), F32)],
        compiler_params=pltpu.CompilerParams(
            dimension_semantics=("parallel", "arbitrary"), vmem_limit_bytes=VMEM_LIMIT_BYTES),
        name="ret",
    )(z, cos2, sin2, *args)


def _ffn_kernel(x_ref, yr_ref, yt_ref, gta_ref, shf_ref, scf_ref, gtf_ref, wout_ref, gnf_ref,
                wup_ref, cw_ref, cb_ref, wdn_ref, gfin_ref, o_ref, carry_ref, acc_ref):
    tm = x_ref.shape[1]

    @pl.when(pl.program_id(1) == 0)
    def _():
        carry_ref[...] = jnp.zeros_like(carry_ref)

    ymix = (jnp.dot(yr_ref[0].astype(BF16), wout_ref[0:RWKV_WIDTH, :], preferred_element_type=F32)
            + jnp.dot(yt_ref[0].astype(BF16), wout_ref[RWKV_WIDTH:, :], preferred_element_type=F32))
    x1 = x_ref[0] + gta_ref[0] * ymix
    h = (_rms(x1) * gnf_ref[...] * (1.0 + scf_ref[0]) + shf_ref[0]).astype(BF16)
    rowi = lax.broadcasted_iota(jnp.int32, (tm, 1), 0)

    def conv(u, col):
        tf = u.shape[1]
        prev = carry_ref[:, col:col + tf]
        u1 = jnp.where(rowi == 0, prev[7:8, :], pltpu.roll(u, 1, 0))
        u2 = jnp.where(rowi == 0, prev[6:7, :], jnp.where(rowi == 1, prev[7:8, :], pltpu.roll(u, 2, 0)))
        carry_ref[:, col:col + tf] = u[tm - 8:tm, :]
        cw = cw_ref[:, col:col + tf]
        return cb_ref[:, col:col + tf] + u2 * cw[0:1, :] + u1 * cw[1:2, :] + u * cw[2:3, :]

    tf = FFN_COLS
    for n, j in enumerate(range(0, D_FF, tf)):
        val = conv(jnp.dot(h, wup_ref[:, j:j + tf], preferred_element_type=F32), j)
        gate = conv(jnp.dot(h, wup_ref[:, D_FF + j:D_FF + j + tf], preferred_element_type=F32), D_FF + j)
        act = (gate * _sigmoid(gate) * val).astype(BF16)
        part = jnp.dot(act, wdn_ref[j:j + tf, :], preferred_element_type=F32)
        if n == 0:
            acc_ref[...] = part
        else:
            acc_ref[...] += part
    x2 = x1 + gtf_ref[0] * acc_ref[...]
    o_ref[0] = _rms(x2) * gfin_ref[...]


def _ffn(x, y_rwkv, y_ret, gt_a, sh_f, sc_f, gt_f, w_out, g_ffn, w_up, conv_w, conv_b, w_down, g_fin):
    B, S, D = x.shape
    tm = FFN_ROWS
    row = lambda b, t: (b, t, 0)
    per_b = lambda b, t: (b, 0, 0)
    const = lambda b, t: (0, 0)
    once = lambda a: pl.BlockSpec(a.shape, const, pipeline_mode=pl.Buffered(1))
    vecs = [g_ffn.reshape(1, D)]
    return pl.pallas_call(
        _ffn_kernel,
        grid=(B, S // tm),
        in_specs=[pl.BlockSpec((1, tm, D), row),
                  pl.BlockSpec((1, tm, RWKV_WIDTH), row),
                  pl.BlockSpec((1, tm, RET_WIDTH), row),
                  pl.BlockSpec((1, 1, D), per_b), pl.BlockSpec((1, 1, D), per_b),
                  pl.BlockSpec((1, 1, D), per_b), pl.BlockSpec((1, 1, D), per_b),
                  once(w_out), once(vecs[0]), once(w_up), once(conv_w), pl.BlockSpec((1, 2 * D_FF), const),
                  once(w_down), pl.BlockSpec((1, D), const)],
        out_specs=pl.BlockSpec((1, tm, D), row),
        out_shape=jax.ShapeDtypeStruct((B, S, D), F32),
        scratch_shapes=[pltpu.VMEM((8, 2 * D_FF), F32),
                        pltpu.VMEM((tm, D), F32)],
        compiler_params=pltpu.CompilerParams(
            dimension_semantics=("parallel", "arbitrary"), vmem_limit_bytes=VMEM_LIMIT_BYTES),
        name="ffn",
    )(x, y_rwkv, y_ret, gt_a, sh_f, sc_f, gt_f, w_out, vecs[0], w_up, conv_w,
      conv_b.reshape(1, 2 * D_FF), w_down, g_fin.reshape(1, D))


def _tables(S):
    half = HEAD_DIM // 2
    pos = jnp.arange(S, dtype=F32)
    inv_freq = ROPE_BASE ** (-jnp.arange(0, HEAD_DIM, 2, dtype=F32) / HEAD_DIM)
    ang = pos[:, None] * inv_freq[None, :]
    cos, sin = jnp.cos(ang), jnp.sin(ang)
    cos2 = jnp.concatenate([cos, cos, cos, cos], axis=1)
    sin2 = jnp.concatenate([-sin, sin, -sin, sin], axis=1)
    log_gamma = jnp.log1p(-(2.0 ** (-5.0 - jnp.arange(N_HEADS, dtype=F32))))
    idx = jnp.arange(CHUNK, dtype=F32)
    dmat = jnp.exp(log_gamma[:, None, None] * jnp.abs(idx[:, None] - idx[None, :]))
    q_dec = jnp.exp(log_gamma[:, None] * (idx + 1.0))
    k_dec = jnp.exp(log_gamma[:, None] * (CHUNK - 1.0 - idx))
    widen = lambda t: jnp.repeat(t.T, HEAD_DIM, axis=1)
    gamma_chunk = jnp.exp(log_gamma * CHUNK)
    gc = jnp.broadcast_to(gamma_chunk[:, None, None], (N_HEADS, 1, HEAD_DIM))
    head = jnp.arange(RWKV_WIDTH) // HEAD_DIM
    seg = (head[:, None] == head[None, :]).astype(BF16)
    return cos2, sin2, dmat, widen(q_dec), widen(k_dec), gc, seg


def kernel(x, c, w_ada, b_ada, attn_norm_g, w_in, rwkv_mu, rwkv_w0, rwkv_w2, rwkv_a0, rwkv_a2, rwkv_g2,
           rwkv_k_k, rwkv_k_a, rwkv_r_k, rwkv_ln_g, rwkv_ln_b, ret_gn_g, w_out, ffn_norm_g, ffn_w_up,
           ffn_conv_w, ffn_conv_b, ffn_w_down, final_norm_g):
    B, S, D = x.shape
    assert D == D_MODEL and S % FFN_ROWS == 0 and w_ada.shape[0] == 1
    cos2, sin2, dmat, qdec, kdec, gc, seg = _tables(S)
    mod = _ada(c, w_ada[0], b_ada[0])
    sh_a, sc_a, gt_a, sh_f, sc_f, gt_f = [m.reshape(B, 1, D) for m in jnp.split(mod, N_MOD, axis=-1)]
    z_rwkv, z_ret = _inproj(x, attn_norm_g[0], sc_a, sh_a, w_in[0].astype(BF16))
    y_rwkv = _rwkv(z_rwkv, rwkv_mu[0], rwkv_w0[0], rwkv_w2[0], rwkv_a0[0], rwkv_a2[0], rwkv_g2[0],
                   rwkv_k_k[0], rwkv_k_a[0], rwkv_r_k[0], rwkv_ln_g[0], rwkv_ln_b[0], seg)
    y_ret = _ret(z_ret, cos2, sin2, dmat, qdec, kdec, gc, ret_gn_g[0], seg)
    return _ffn(x, y_rwkv, y_ret, gt_a, sh_f, sc_f, gt_f, w_out[0].astype(BF16), ffn_norm_g[0],
                ffn_w_up[0].astype(BF16), ffn_conv_w[0], ffn_conv_b[0], ffn_w_down[0].astype(BF16),
                final_norm_g)
```

```python
import functools
import math

import jax
import jax.numpy as jnp
from jax import lax
from jax.experimental import pallas as pl
from jax.experimental.pallas import tpu as pltpu

F32 = jnp.float32
BF16 = jnp.bfloat16

D_MODEL = 1024
CHUNK = 64
HEAD_DIM = 64
RWKV_WIDTH = 512
RET_WIDTH = 512
N_HEADS = 8
DECAY_LORA = 64
AAA_LORA = 64
GATE_LORA = 128
RWKV_COLS = 3 * RWKV_WIDTH + DECAY_LORA + AAA_LORA + GATE_LORA
RET_COLS = 4 * RET_WIDTH
D_FF = 2816
ROPE_BASE = 10000.0
NORM_EPS = 1e-6
RWKV_LN_EPS = 64e-5
RET_GN_EPS = 1e-6
W_DECAY_SCALE = math.exp(-0.5)
N_MOD = 6
HEADS_PER_GROUP = 4
GROUP = HEADS_PER_GROUP * HEAD_DIM

VMEM_LIMIT_BYTES = 56 * 1024 * 1024
INPROJ_ROWS = 256
RWKV_TOKENS = 256
RET_TOKENS = 256
FFN_ROWS = 512
FFN_COLS = 256
ADA_COLS = 1536

_NT = (((1,), (1,)), ((), ()))


def _bdot(a, b):
    return jnp.dot(a.astype(BF16), b.astype(BF16), preferred_element_type=F32)


def _split_parts(x, parts=3):
    out, rem = [], x
    for i in range(parts):
        p = rem.astype(BF16)
        out.append(p)
        if i + 1 < parts:
            rem = rem - p.astype(F32)
    return out


def _head_sums(x, mask):
    outs = []
    for g in range(x.shape[1] // GROUP):
        acc = None
        for p in _split_parts(x[:, g * GROUP:(g + 1) * GROUP], parts=2):
            d = jnp.dot(p, mask, preferred_element_type=F32)
            acc = d if acc is None else acc + d
        outs.append(acc)
    return jnp.concatenate(outs, axis=1)


def _exact_dot_left(m_bf16, x):
    acc = None
    for p in _split_parts(x):
        d = jnp.dot(m_bf16, p, preferred_element_type=F32)
        acc = d if acc is None else acc + d
    return acc


def _sigmoid(x):
    return 1.0 / (1.0 + jnp.exp(-x))


def _rms(x):
    return x * lax.rsqrt(jnp.mean(x * x, axis=-1, keepdims=True) + NORM_EPS)


def _ada_kernel(c_ref, w_ref, b_ref, o_ref):
    cv = c_ref[...]
    s = cv * _sigmoid(cv)
    o_ref[...] = jnp.dot(s, w_ref[...], preferred_element_type=F32,
                         precision=lax.Precision.HIGHEST) + b_ref[...]


def _ada(c, w, b):
    B = c.shape[0]
    n = w.shape[1]
    return pl.pallas_call(
        _ada_kernel,
        grid=(n // ADA_COLS,),
        in_specs=[pl.BlockSpec((B, D_MODEL), lambda j: (0, 0)),
                  pl.BlockSpec((D_MODEL, ADA_COLS), lambda j: (0, j)),
                  pl.BlockSpec((1, ADA_COLS), lambda j: (0, j))],
        out_specs=pl.BlockSpec((B, ADA_COLS), lambda j: (0, j)),
        out_shape=jax.ShapeDtypeStruct((B, n), F32),
        compiler_params=pltpu.CompilerParams(vmem_limit_bytes=VMEM_LIMIT_BYTES),
        name="ada",
    )(c, w, b.reshape(1, n))


def _inproj_kernel(x_ref, g_ref, sc_ref, sh_ref, w_ref, zr_ref, zt_ref):
    h = (_rms(x_ref[0]) * g_ref[...] * (1.0 + sc_ref[0]) + sh_ref[0]).astype(BF16)
    step = 256
    for j in range(0, RWKV_COLS, step):
        zr_ref[0, :, j:j + step] = jnp.dot(h, w_ref[:, j:j + step], preferred_element_type=F32)
    for j in range(0, RET_COLS, step):
        zt_ref[0, :, j:j + step] = jnp.dot(h, w_ref[:, RWKV_COLS + j:RWKV_COLS + j + step],
                                           preferred_element_type=F32)


def _inproj(x, g, sc, sh, w_bf16):
    B, S, D = x.shape
    tm = INPROJ_ROWS
    row = lambda b, t: (b, t, 0)
    per_b = lambda b, t: (b, 0, 0)
    const = lambda b, t: (0, 0)
    return pl.pallas_call(
        _inproj_kernel,
        grid=(B, S // tm),
        in_specs=[pl.BlockSpec((1, tm, D), row),
                  pl.BlockSpec((1, D), const),
                  pl.BlockSpec((1, 1, D), per_b),
                  pl.BlockSpec((1, 1, D), per_b),
                  pl.BlockSpec(w_bf16.shape, const)],
        out_specs=[pl.BlockSpec((1, tm, RWKV_COLS), row),
                   pl.BlockSpec((1, tm, RET_COLS), row)],
        out_shape=[jax.ShapeDtypeStruct((B, S, RWKV_COLS), F32),
                   jax.ShapeDtypeStruct((B, S, RET_COLS), F32)],
        compiler_params=pltpu.CompilerParams(
            dimension_semantics=("parallel", "arbitrary"), vmem_limit_bytes=VMEM_LIMIT_BYTES),
        name="inproj",
    )(x, g.reshape(1, D), sc, sh, w_bf16)


def _head_norm(y, mask, eps):
    mean = _head_sums(y, mask) * (1.0 / HEAD_DIM)
    d = y - mean
    var = _head_sums(d * d, mask) * (1.0 / HEAD_DIM)
    return d * lax.rsqrt(var + eps)


def _bd(x, mask):
    xb = x.astype(BF16)
    half = GROUP // 2
    zeros = jnp.zeros((HEAD_DIM, half), BF16)
    blocks = []
    for h in range(HEADS_PER_GROUP):
        rows = slice(h * HEAD_DIM, (h + 1) * HEAD_DIM)
        if h < HEADS_PER_GROUP // 2:
            blocks.append(jnp.concatenate([xb[:, 0:half] * mask[rows, 0:half], zeros], axis=1))
        else:
            blocks.append(jnp.concatenate([zeros, xb[:, half:GROUP] * mask[rows, half:GROUP]], axis=1))
    return jnp.concatenate(blocks, axis=0)


def _block_transpose(x):
    r = jnp.concatenate([x] * HEADS_PER_GROUP, axis=0).T
    blk = lax.broadcasted_iota(jnp.int32, (1, GROUP), 1) // HEAD_DIM
    out = r[0:HEAD_DIM]
    for h in range(1, HEADS_PER_GROUP):
        out = jnp.where(blk == h, r[h * HEAD_DIM:(h + 1) * HEAD_DIM], out)
    return out


def _wdot(lhs, rhs_bd):
    return jnp.dot(lhs.astype(BF16), rhs_bd, preferred_element_type=F32)


def _wdot_nt(lhs, rhs_bd):
    return lax.dot_general(lhs.astype(BF16), rhs_bd, _NT, preferred_element_type=F32)


def _rwkv_kernel(z_ref, mu_ref, w0_ref, w2_ref, a0_ref, a2_ref, g2_ref, kk_ref, ka_ref, rk_ref,
                 lng_ref, lnb_ref, mask_ref, y_ref, zlast_ref, state_ref, ybuf_ref):
    C = CHUNK
    W = RWKV_WIDTH
    T = z_ref.shape[1]
    n_chunks = T // C
    n_groups = W // GROUP

    @pl.when(pl.program_id(1) == 0)
    def _():
        zlast_ref[...] = jnp.zeros_like(zlast_ref)
        state_ref[...] = jnp.zeros_like(state_ref)

    z = z_ref[0]
    row = lax.broadcasted_iota(jnp.int32, (T, 1), 0)
    zprev = jnp.where(row == 0, zlast_ref[0:1, :], pltpu.roll(z, 1, 0))
    zlast_ref[0:1, :] = z[T - 1:T, :]
    zm = z + mu_ref[...] * (zprev - z)
    r = zm[:, 0:W]
    k = zm[:, W:2 * W]
    v = zm[:, 2 * W:3 * W]
    wd = zm[:, 3 * W:3 * W + DECAY_LORA]
    ad = zm[:, 3 * W + DECAY_LORA:3 * W + DECAY_LORA + AAA_LORA]
    gd = zm[:, 3 * W + DECAY_LORA + AAA_LORA:RWKV_COLS]

    lw = -W_DECAY_SCALE * _sigmoid(w0_ref[...] + _bdot(jnp.tanh(wd), w2_ref[...]))
    a = _sigmoid(a0_ref[...] + _bdot(ad, a2_ref[...]))
    g = _bdot(_sigmoid(gd), g2_ref[...])

    mask = mask_ref[...]
    kkr = k * kk_ref[...]
    nrm = jnp.sqrt(_head_sums(kkr * kkr, mask))
    kkn = kkr / jnp.maximum(nrm, 1e-12)
    km = k * (1.0 + (a - 1.0) * ka_ref[...])
    kb = kkn * a

    ri = lax.broadcasted_iota(jnp.int32, (C, C), 0)
    ci = lax.broadcasted_iota(jnp.int32, (C, C), 1)
    tri = (ri >= ci).astype(BF16)
    rw = lax.broadcasted_iota(jnp.int32, (C, GROUP), 0)
    cw = lax.broadcasted_iota(jnp.int32, (C, GROUP), 1) % HEAD_DIM
    strict = rw > cw
    incl = rw >= cw
    eye = (rw == cw).astype(F32)

    units = []
    for c in range(n_chunks):
        rows = slice(c * C, (c + 1) * C)
        lw_c = lw[rows]
        lp = _exact_dot_left(tri, lw_c)
        rt = r[rows] * jnp.exp(lp)
        at = -kkn[rows] * jnp.exp(lp - lw_c)
        em = jnp.exp(-lp)
        bt = kb[rows] * em
        kt = km[rows] * em
        lpc = lp[C - 1:C, :]
        ee = jnp.exp(lpc - lp)
        b_d = kb[rows] * ee
        k_d = km[rows] * ee
        pc = jnp.broadcast_to(jnp.exp(lpc), (C, W))
        for gi in range(n_groups):
            ln = slice(gi * GROUP, (gi + 1) * GROUP)
            units.append(dict(c=c, g=gi, at=at[:, ln], rt=rt[:, ln], bt=bt[:, ln], kt=kt[:, ln],
                              v=v[rows, ln], b_d=b_d[:, ln], k_d=k_d[:, ln], pc=pc[:, ln]))
    for u in units:
        lhs = jnp.concatenate([u["at"], u["rt"]], axis=0)
        sb = _wdot_nt(lhs, _bd(u["bt"], mask))
        sk = _wdot_nt(lhs, _bd(u["kt"], mask))
        u["p"] = jnp.where(strict, sb[0:C], 0.0)
        u["m_rb"] = jnp.where(incl, sb[C:2 * C], 0.0)
        u["l_ak"] = jnp.where(strict, sk[0:C], 0.0)
        u["m_rk"] = jnp.where(incl, sk[C:2 * C], 0.0)
        u["tinv"] = eye + u["p"]
        u["bt_b"] = _block_transpose(u["b_d"])
        u["bt_k"] = _block_transpose(u["k_d"])
        u["pcw"] = _block_transpose(u["pc"])
    for u in units:
        u["p"] = _wdot(u["p"], _bd(u["p"], mask))
    for j in range(1, 6):
        for u in units:
            pbd = _bd(u["p"], mask)
            if j < 5:
                both = _wdot(jnp.concatenate([u["p"], u["tinv"]], axis=0), pbd)
                u["p"] = both[0:C]
                u["tinv"] = u["tinv"] + both[C:2 * C]
            else:
                u["tinv"] = u["tinv"] + _wdot(u["tinv"], pbd)
    for u in units:
        rv = _wdot(jnp.concatenate([u["l_ak"], u["m_rk"], u["bt_k"]], axis=0), _bd(u["v"], mask))
        u["x"], u["y0"], u["psi"] = rv[0:C], rv[C:2 * C], rv[2 * C:3 * C]
    for u in units:
        u["w"] = _wdot(u["tinv"], _bd(u["at"], mask))
        u["u0"] = _wdot(u["tinv"], _bd(u["x"], mask))
    for u in units:
        lhs = jnp.concatenate([u["bt_b"], u["m_rb"]], axis=0)
        rw_ = _wdot(lhs, _bd(u["w"], mask))
        ru = _wdot(lhs, _bd(u["u0"], mask))
        u["po"] = jnp.concatenate([rw_[0:C], u["rt"] + rw_[C:2 * C]], axis=0)
        u["psi"] = u["psi"] + ru[0:C]
        u["y0"] = u["y0"] + ru[C:2 * C]

    hcur = [state_ref[gi] for gi in range(n_groups)]
    for u in units:
        c, gi = u["c"], u["g"]
        ph = _wdot(u["po"], _bd(hcur[gi], mask))
        ybuf_ref[c * C:(c + 1) * C, gi * GROUP:(gi + 1) * GROUP] = ph[C:2 * C] + u["y0"]
        hcur[gi] = u["pcw"] * hcur[gi] + ph[0:C] + u["psi"]
    for gi in range(n_groups):
        state_ref[gi] = hcur[gi]

    y = _head_norm(ybuf_ref[...], mask, RWKV_LN_EPS) * lng_ref[...] + lnb_ref[...]
    bonus = _head_sums(r * km * rk_ref[...], mask) * v
    y_ref[0] = (y + bonus) * g


def _rwkv(z, mu, w0, w2, a0, a2, g2, k_k, k_a, r_k, ln_g, ln_b, mask):
    B, S, _ = z.shape
    W = RWKV_WIDTH
    T = RWKV_TOKENS
    row = lambda b, t: (b, t, 0)
    const = lambda b, t: (0, 0)
    vec = lambda a: a.reshape(1, -1)
    full = lambda a: pl.BlockSpec(a.shape, const)
    args = [vec(mu), vec(w0), w2.astype(BF16), vec(a0), a2.astype(BF16), g2.astype(BF16),
            vec(k_k), vec(k_a), vec(r_k), vec(ln_g), vec(ln_b), mask]
    return pl.pallas_call(
        _rwkv_kernel,
        grid=(B, S // T),
        in_specs=[pl.BlockSpec((1, T, RWKV_COLS), row)] + [full(a) for a in args],
        out_specs=pl.BlockSpec((1, T, W), row),
        out_shape=jax.ShapeDtypeStruct((B, S, W), F32),
        scratch_shapes=[pltpu.VMEM((8, RWKV_COLS), F32),
                        pltpu.VMEM((W // GROUP, HEAD_DIM, GROUP), F32),
                        pltpu.VMEM((T, W), F32)],
        compiler_params=pltpu.CompilerParams(
            dimension_semantics=("parallel", "arbitrary"), vmem_limit_bytes=VMEM_LIMIT_BYTES),
        name="rwkv",
    )(z, *args)


def _ret_kernel(z_ref, cos_ref, sin_ref, dm_ref, qdec_ref, kdec_ref, gc_ref, gng_ref, mask_ref,
                y_ref, state_ref, ybuf_ref):
    C = CHUNK
    W = RET_WIDTH
    T = z_ref.shape[1]
    n_chunks = T // C
    n_groups = W // GROUP

    @pl.when(pl.program_id(1) == 0)
    def _():
        state_ref[...] = jnp.zeros_like(state_ref)

    z = z_ref[0]
    q = z[:, 0:W]
    k = z[:, W:2 * W]
    v = z[:, 2 * W:3 * W]
    g = z[:, 3 * W:4 * W]
    reps = W // cos_ref.shape[1]
    cosf = jnp.concatenate([cos_ref[...]] * reps, axis=1)
    sinf = jnp.concatenate([sin_ref[...]] * reps, axis=1)
    lane = lax.broadcasted_iota(jnp.int32, (1, W), 1)
    first_half = (lane % HEAD_DIM) < (HEAD_DIM // 2)

    def rot(t):
        swapped = jnp.where(first_half, pltpu.roll(t, W - HEAD_DIM // 2, 1), pltpu.roll(t, HEAD_DIM // 2, 1))
        return t * cosf + swapped * sinf

    q = rot(q) * (HEAD_DIM ** -0.5)
    k = rot(k)
    mask = mask_ref[...]

    units = []
    for c in range(n_chunks):
        rows = slice(c * C, (c + 1) * C)
        qd = q[rows] * qdec_ref[...]
        kd = k[rows] * kdec_ref[...]
        for gi in range(n_groups):
            ln = slice(gi * GROUP, (gi + 1) * GROUP)
            units.append(dict(c=c, g=gi, q=q[rows, ln], k=k[rows, ln], v=v[rows, ln], qd=qd[:, ln], kd=kd[:, ln]))
    for u in units:
        u["sc"] = _wdot_nt(u["q"], _bd(u["k"], mask)) * dm_ref[:, u["g"] * GROUP:(u["g"] + 1) * GROUP]
        u["kdt"] = _block_transpose(u["kd"])
    for u in units:
        both = _wdot(jnp.concatenate([u["sc"], u["kdt"]], axis=0), _bd(u["v"], mask))
        u["y"], u["kv"] = both[0:C], both[C:2 * C]

    scur = [state_ref[gi] for gi in range(n_groups)]
    for u in units:
        c, gi = u["c"], u["g"]
        ln = slice(gi * GROUP, (gi + 1) * GROUP)
        ybuf_ref[c * C:(c + 1) * C, ln] = u["y"] + _wdot(u["qd"], _bd(scur[gi], mask))
        scur[gi] = scur[gi] * gc_ref[:, ln] + u["kv"]
    for gi in range(n_groups):
        state_ref[gi] = scur[gi]

    y = _head_norm(ybuf_ref[...], mask, RET_GN_EPS) * gng_ref[...]
    y_ref[0] = g * _sigmoid(g) * y


def _ret(z, cos2, sin2, dmw, qdec, kdec, gcw, gn_g, mask):
    B, S, _ = z.shape
    W = RET_WIDTH
    T = RET_TOKENS
    row = lambda b, t: (b, t, 0)
    args = [dmw, qdec, kdec, gcw, gn_g.reshape(1, W), mask]
    full = lambda a: pl.BlockSpec(a.shape, lambda b, t: (0,) * a.ndim)
    return pl.pallas_call(
        _ret_kernel,
        grid=(B, S // T),
        in_specs=[pl.BlockSpec((1, T, RET_COLS), row),
                  pl.BlockSpec((T, cos2.shape[1]), lambda b, t: (t, 0)),
                  pl.BlockSpec((T, sin2.shape[1]), lambda b, t: (t, 0))] + [full(a) for a in args],
        out_specs=pl.BlockSpec((1, T, W), row),
        out_shape=jax.ShapeDtypeStruct((B, S, W), F32),
        scratch_shapes=[pltpu.VMEM((W // GROUP, HEAD_DIM, GROUP), F32),
                        pltpu.VMEM((T, W), F32)],
        compiler_params=pltpu.CompilerParams(
            dimension_semantics=("parallel", "arbitrary"), vmem_limit_bytes=VMEM_LIMIT_BYTES),
        name="ret",
    )(z, cos2, sin2, *args)


def _ffn_kernel(x_ref, yr_ref, yt_ref, gta_ref, shf_ref, scf_ref, gtf_ref, wout_ref, gnf_ref,
                wup_ref, cw_ref, cb_ref, wdn_ref, gfin_ref, o_ref, carry_ref, ubuf_ref, abuf_ref):
    tm = x_ref.shape[1]

    @pl.when(pl.program_id(1) == 0)
    def _():
        carry_ref[...] = jnp.zeros_like(carry_ref)

    ymix = (jnp.dot(yr_ref[0].astype(BF16), wout_ref[0:RWKV_WIDTH, :], preferred_element_type=F32)
            + jnp.dot(yt_ref[0].astype(BF16), wout_ref[RWKV_WIDTH:, :], preferred_element_type=F32))
    x1 = x_ref[0] + gta_ref[0] * ymix
    h = (_rms(x1) * gnf_ref[...] * (1.0 + scf_ref[0]) + shf_ref[0]).astype(BF16)
    rowi = lax.broadcasted_iota(jnp.int32, (tm, 1), 0)

    def conv(u, col):
        tf = u.shape[1]
        prev = carry_ref[:, col:col + tf]
        u1 = jnp.where(rowi == 0, prev[7:8, :], pltpu.roll(u, 1, 0))
        u2 = jnp.where(rowi == 0, prev[6:7, :], jnp.where(rowi == 1, prev[7:8, :], pltpu.roll(u, 2, 0)))
        carry_ref[:, col:col + tf] = u[tm - 8:tm, :]
        cw = cw_ref[:, col:col + tf]
        return cb_ref[:, col:col + tf] + u2 * cw[0:1, :] + u1 * cw[1:2, :] + u * cw[2:3, :]

    tf = FFN_COLS
    tiles = list(range(0, D_FF, tf))

    def up(n):
        j, slot = tiles[n], n % 2
        ubuf_ref[slot, :, 0:tf] = jnp.dot(h, wup_ref[:, j:j + tf], preferred_element_type=F32)
        ubuf_ref[slot, :, tf:2 * tf] = jnp.dot(h, wup_ref[:, D_FF + j:D_FF + j + tf], preferred_element_type=F32)

    def glu(n):
        j, slot = tiles[n], n % 2
        val = conv(ubuf_ref[slot, :, 0:tf], j)
        gate = conv(ubuf_ref[slot, :, tf:2 * tf], D_FF + j)
        abuf_ref[:, j:j + tf] = (gate * _sigmoid(gate) * val).astype(BF16)

    up(0)
    for n in range(1, len(tiles)):
        up(n)
        glu(n - 1)
    glu(len(tiles) - 1)
    y = jnp.dot(abuf_ref[...], wdn_ref[...], preferred_element_type=F32)
    x2 = x1 + gtf_ref[0] * y
    o_ref[0] = _rms(x2) * gfin_ref[...]


def _ffn(x, y_rwkv, y_ret, gt_a, sh_f, sc_f, gt_f, w_out, g_ffn, w_up, conv_w, conv_b, w_down, g_fin):
    B, S, D = x.shape
    tm = FFN_ROWS
    row = lambda b, t: (b, t, 0)
    per_b = lambda b, t: (b, 0, 0)
    const = lambda b, t: (0, 0)
    once = lambda a: pl.BlockSpec(a.shape, const, pipeline_mode=pl.Buffered(1))
    vecs = [g_ffn.reshape(1, D)]
    return pl.pallas_call(
        _ffn_kernel,
        grid=(B, S // tm),
        in_specs=[pl.BlockSpec((1, tm, D), row),
                  pl.BlockSpec((1, tm, RWKV_WIDTH), row),
                  pl.BlockSpec((1, tm, RET_WIDTH), row),
                  pl.BlockSpec((1, 1, D), per_b), pl.BlockSpec((1, 1, D), per_b),
                  pl.BlockSpec((1, 1, D), per_b), pl.BlockSpec((1, 1, D), per_b),
                  once(w_out), once(vecs[0]), once(w_up), once(conv_w), pl.BlockSpec((1, 2 * D_FF), const),
                  once(w_down), pl.BlockSpec((1, D), const)],
        out_specs=pl.BlockSpec((1, tm, D), row),
        out_shape=jax.ShapeDtypeStruct((B, S, D), F32),
        scratch_shapes=[pltpu.VMEM((8, 2 * D_FF), F32),
                        pltpu.VMEM((2, tm, 2 * FFN_COLS), F32),
                        pltpu.VMEM((tm, D_FF), BF16)],
        compiler_params=pltpu.CompilerParams(
            dimension_semantics=("parallel", "arbitrary"), vmem_limit_bytes=VMEM_LIMIT_BYTES),
        name="ffn",
    )(x, y_rwkv, y_ret, gt_a, sh_f, sc_f, gt_f, w_out, vecs[0], w_up, conv_w,
      conv_b.reshape(1, 2 * D_FF), w_down, g_fin.reshape(1, D))


def _tables(S):
    pos = jnp.arange(S, dtype=F32)
    inv_freq = ROPE_BASE ** (-jnp.arange(0, HEAD_DIM, 2, dtype=F32) / HEAD_DIM)
    ang = pos[:, None] * inv_freq[None, :]
    cos, sin = jnp.cos(ang), jnp.sin(ang)
    cos2 = jnp.concatenate([cos, cos, cos, cos], axis=1)
    sin2 = jnp.concatenate([-sin, sin, -sin, sin], axis=1)
    log_gamma = jnp.log1p(-(2.0 ** (-5.0 - jnp.arange(N_HEADS, dtype=F32))))
    idx = jnp.arange(CHUNK, dtype=F32)
    dmat = jnp.exp(log_gamma[:, None, None] * jnp.abs(idx[:, None] - idx[None, :]))
    q_dec = jnp.exp(log_gamma[:, None] * (idx + 1.0))
    k_dec = jnp.exp(log_gamma[:, None] * (CHUNK - 1.0 - idx))
    widen = lambda t: jnp.repeat(t.T, HEAD_DIM, axis=1)
    dmw = dmat.transpose(1, 0, 2).reshape(CHUNK, N_HEADS * CHUNK)
    gcw = jnp.repeat(jnp.exp(log_gamma * CHUNK), HEAD_DIM)[None, :]
    head = jnp.arange(GROUP) // HEAD_DIM
    mask = (head[:, None] == head[None, :]).astype(BF16)
    return cos2, sin2, dmw, widen(q_dec), widen(k_dec), gcw, mask


def kernel(x, c, w_ada, b_ada, attn_norm_g, w_in, rwkv_mu, rwkv_w0, rwkv_w2, rwkv_a0, rwkv_a2, rwkv_g2,
           rwkv_k_k, rwkv_k_a, rwkv_r_k, rwkv_ln_g, rwkv_ln_b, ret_gn_g, w_out, ffn_norm_g, ffn_w_up,
           ffn_conv_w, ffn_conv_b, ffn_w_down, final_norm_g):
    B, S, D = x.shape
    assert D == D_MODEL and S % FFN_ROWS == 0 and w_ada.shape[0] == 1
    cos2, sin2, dmw, qdec, kdec, gcw, mask = _tables(S)
    mod = _ada(c, w_ada[0], b_ada[0])
    sh_a, sc_a, gt_a, sh_f, sc_f, gt_f = [m.reshape(B, 1, D) for m in jnp.split(mod, N_MOD, axis=-1)]
    z_rwkv, z_ret = _inproj(x, attn_norm_g[0], sc_a, sh_a, w_in[0].astype(BF16))
    y_rwkv = _rwkv(z_rwkv, rwkv_mu[0], rwkv_w0[0], rwkv_w2[0], rwkv_a0[0], rwkv_a2[0], rwkv_g2[0],
                   rwkv_k_k[0], rwkv_k_a[0], rwkv_r_k[0], rwkv_ln_g[0], rwkv_ln_b[0], mask)
    y_ret = _ret(z_ret, cos2, sin2, dmw, qdec, kdec, gcw, ret_gn_g[0], mask)
    return _ffn(x, y_rwkv, y_ret, gt_a, sh_f, sc_f, gt_f, w_out[0].astype(BF16), ffn_norm_g[0],
                ffn_w_up[0].astype(BF16), ffn_conv_w[0], ffn_conv_b[0], ffn_w_down[0].astype(BF16),
                final_norm_g)
```

```python
import functools
import math

import jax
import jax.numpy as jnp
from jax import lax
from jax.experimental import pallas as pl
from jax.experimental.pallas import tpu as pltpu

F32 = jnp.float32
BF16 = jnp.bfloat16

D_MODEL = 1024
CHUNK = 64
HEAD_DIM = 64
RWKV_WIDTH = 512
RET_WIDTH = 512
N_HEADS = 8
DECAY_LORA = 64
AAA_LORA = 64
GATE_LORA = 128
RWKV_COLS = 3 * RWKV_WIDTH + DECAY_LORA + AAA_LORA + GATE_LORA
RET_COLS = 4 * RET_WIDTH
D_FF = 2816
ROPE_BASE = 10000.0
NORM_EPS = 1e-6
RWKV_LN_EPS = 64e-5
RET_GN_EPS = 1e-6
W_DECAY_SCALE = math.exp(-0.5)
N_MOD = 6
HEADS_PER_GROUP = 4
GROUP = HEADS_PER_GROUP * HEAD_DIM

VMEM_LIMIT_BYTES = 56 * 1024 * 1024
INPROJ_ROWS = 256
RWKV_TOKENS = 512
RET_TOKENS = 512
FFN_ROWS = 512
FFN_COLS = 256
ADA_COLS = 1536

_NT = (((1,), (1,)), ((), ()))


def _bdot(a, b):
    return jnp.dot(a.astype(BF16), b.astype(BF16), preferred_element_type=F32)


def _split_parts(x, parts=3):
    out, rem = [], x
    for i in range(parts):
        p = rem.astype(BF16)
        out.append(p)
        if i + 1 < parts:
            rem = rem - p.astype(F32)
    return out


def _head_sums(x, mask, parts=2):
    outs = []
    for g in range(x.shape[1] // GROUP):
        acc = None
        for p in _split_parts(x[:, g * GROUP:(g + 1) * GROUP], parts=parts):
            d = jnp.dot(p, mask, preferred_element_type=F32)
            acc = d if acc is None else acc + d
        outs.append(acc)
    return jnp.concatenate(outs, axis=1)


def _exact_dot_left(m_bf16, x):
    acc = None
    for p in _split_parts(x):
        d = jnp.dot(m_bf16, p, preferred_element_type=F32)
        acc = d if acc is None else acc + d
    return acc


def _sigmoid(x):
    return 1.0 / (1.0 + jnp.exp(-x))


def _rms(x):
    return x * lax.rsqrt(jnp.mean(x * x, axis=-1, keepdims=True) + NORM_EPS)


def _ada_kernel(c_ref, w_ref, b_ref, o_ref):
    cv = c_ref[...]
    s = cv * _sigmoid(cv)
    o_ref[...] = jnp.dot(s, w_ref[...], preferred_element_type=F32,
                         precision=lax.Precision.HIGHEST) + b_ref[...]


def _ada(c, w, b):
    B = c.shape[0]
    n = w.shape[1]
    return pl.pallas_call(
        _ada_kernel,
        grid=(n // ADA_COLS,),
        in_specs=[pl.BlockSpec((B, D_MODEL), lambda j: (0, 0)),
                  pl.BlockSpec((D_MODEL, ADA_COLS), lambda j: (0, j)),
                  pl.BlockSpec((1, ADA_COLS), lambda j: (0, j))],
        out_specs=pl.BlockSpec((B, ADA_COLS), lambda j: (0, j)),
        out_shape=jax.ShapeDtypeStruct((B, n), F32),
        compiler_params=pltpu.CompilerParams(vmem_limit_bytes=VMEM_LIMIT_BYTES),
        name="ada",
    )(c, w, b.reshape(1, n))


def _inproj_kernel(x_ref, g_ref, sc_ref, sh_ref, w_ref, zr_ref, zt_ref):
    h = (_rms(x_ref[0]) * g_ref[...] * (1.0 + sc_ref[0]) + sh_ref[0]).astype(BF16)
    step = 256
    for j in range(0, RWKV_COLS, step):
        zr_ref[0, :, j:j + step] = jnp.dot(h, w_ref[:, j:j + step], preferred_element_type=F32)
    for j in range(0, RET_COLS, step):
        zt_ref[0, :, j:j + step] = jnp.dot(h, w_ref[:, RWKV_COLS + j:RWKV_COLS + j + step],
                                           preferred_element_type=F32)


def _inproj(x, g, sc, sh, w_bf16):
    B, S, D = x.shape
    tm = INPROJ_ROWS
    row = lambda b, t: (b, t, 0)
    per_b = lambda b, t: (b, 0, 0)
    const = lambda b, t: (0, 0)
    return pl.pallas_call(
        _inproj_kernel,
        grid=(B, S // tm),
        in_specs=[pl.BlockSpec((1, tm, D), row),
                  pl.BlockSpec((1, D), const),
                  pl.BlockSpec((1, 1, D), per_b),
                  pl.BlockSpec((1, 1, D), per_b),
                  pl.BlockSpec(w_bf16.shape, const)],
        out_specs=[pl.BlockSpec((1, tm, RWKV_COLS), row),
                   pl.BlockSpec((1, tm, RET_COLS), row)],
        out_shape=[jax.ShapeDtypeStruct((B, S, RWKV_COLS), F32),
                   jax.ShapeDtypeStruct((B, S, RET_COLS), F32)],
        compiler_params=pltpu.CompilerParams(
            dimension_semantics=("parallel", "arbitrary"), vmem_limit_bytes=VMEM_LIMIT_BYTES),
        name="inproj",
    )(x, g.reshape(1, D), sc, sh, w_bf16)


def _head_norm(y, mask, eps):
    mean = _head_sums(y, mask) * (1.0 / HEAD_DIM)
    d = y - mean
    var = _head_sums(d * d, mask, parts=1) * (1.0 / HEAD_DIM)
    return d * lax.rsqrt(var + eps)


def _shift_rows(x, first):
    rolled = pltpu.roll(x, 1, 0)
    head = jnp.where(lax.broadcasted_iota(jnp.int32, (8, 1), 0) == 0, first, rolled[0:8])
    return jnp.concatenate([head, rolled[8:]], axis=0)


def _bd(x, mask):
    xb = x.astype(BF16)
    half = GROUP // 2
    zeros = jnp.zeros((HEAD_DIM, half), BF16)
    blocks = []
    for h in range(HEADS_PER_GROUP):
        rows = slice(h * HEAD_DIM, (h + 1) * HEAD_DIM)
        if h < HEADS_PER_GROUP // 2:
            blocks.append(jnp.concatenate([xb[:, 0:half] * mask[rows, 0:half], zeros], axis=1))
        else:
            blocks.append(jnp.concatenate([zeros, xb[:, half:GROUP] * mask[rows, half:GROUP]], axis=1))
    return jnp.concatenate(blocks, axis=0)


def _block_transpose(x):
    r = jnp.concatenate([x] * HEADS_PER_GROUP, axis=0).T
    blk = lax.broadcasted_iota(jnp.int32, (1, GROUP), 1) // HEAD_DIM
    out = r[0:HEAD_DIM]
    for h in range(1, HEADS_PER_GROUP):
        out = jnp.where(blk == h, r[h * HEAD_DIM:(h + 1) * HEAD_DIM], out)
    return out


def _wdot(lhs, rhs_bd):
    return jnp.dot(lhs.astype(BF16), rhs_bd, preferred_element_type=F32)


def _wdot_nt(lhs, rhs_bd):
    return lax.dot_general(lhs.astype(BF16), rhs_bd, _NT, preferred_element_type=F32)


_CHUNK_FIELDS = ("at", "rt", "bt", "kt", "v", "b_d", "k_d")


def _rwkv_prep(z, zfirst, prm, mask):
    C, W = CHUNK, RWKV_WIDTH
    T = z.shape[0]
    zprev = _shift_rows(z, zfirst)
    zm = z + prm["mu"] * (zprev - z)
    r = zm[:, 0:W]
    k = zm[:, W:2 * W]
    v = zm[:, 2 * W:3 * W]
    wd = zm[:, 3 * W:3 * W + DECAY_LORA]
    ad = zm[:, 3 * W + DECAY_LORA:3 * W + DECAY_LORA + AAA_LORA]
    gd = zm[:, 3 * W + DECAY_LORA + AAA_LORA:RWKV_COLS]

    lw = -W_DECAY_SCALE * _sigmoid(prm["w0"] + _bdot(jnp.tanh(wd), prm["w2"]))
    a = _sigmoid(prm["a0"] + _bdot(ad, prm["a2"]))
    g = _bdot(_sigmoid(gd), prm["g2"])
    kkr = k * prm["k_k"]
    kkn = kkr * lax.rsqrt(jnp.maximum(_head_sums(kkr * kkr, mask, parts=1), 1e-24))
    km = k * (1.0 + (a - 1.0) * prm["k_a"])
    kb = kkn * a

    ri = lax.broadcasted_iota(jnp.int32, (C, C), 0)
    ci = lax.broadcasted_iota(jnp.int32, (C, C), 1)
    tri = (ri >= ci).astype(BF16)
    out = {name: [] for name in _CHUNK_FIELDS}
    lpcs = []
    for c in range(T // C):
        rows = slice(c * C, (c + 1) * C)
        lw_c = lw[rows]
        lp = _exact_dot_left(tri, lw_c)
        em = jnp.exp(-lp)
        lpc = lp[C - 1:C, :]
        ee = jnp.exp(lpc - lp)
        out["rt"].append(r[rows] * jnp.exp(lp))
        out["at"].append(-kkn[rows] * jnp.exp(lp - lw_c))
        out["bt"].append(kb[rows] * em)
        out["kt"].append(km[rows] * em)
        out["b_d"].append(kb[rows] * ee)
        out["k_d"].append(km[rows] * ee)
        out["v"].append(v[rows])
        lpcs.append(lpc)
    out["bonus"] = _head_sums(r * km * prm["r_k"], mask) * v
    out["g"] = g
    return out, lpcs


def _rwkv_solve(get, lpcs, hcur, mask, n_chunks):
    C = CHUNK
    n_groups = RWKV_WIDTH // GROUP
    rw = lax.broadcasted_iota(jnp.int32, (C, GROUP), 0)
    cw = lax.broadcasted_iota(jnp.int32, (C, GROUP), 1) % HEAD_DIM
    strict = rw > cw
    incl = rw >= cw
    eye = (rw == cw).astype(F32)

    units = []
    for c in range(n_chunks):
        fields = {name: get(name, c) for name in _CHUNK_FIELDS}
        pc = jnp.broadcast_to(jnp.exp(lpcs[c]), (C, RWKV_WIDTH))
        for gi in range(n_groups):
            ln = slice(gi * GROUP, (gi + 1) * GROUP)
            u = {name: val[:, ln] for name, val in fields.items()}
            u.update(c=c, g=gi, pc=pc[:, ln])
            units.append(u)
    for u in units:
        lhs = jnp.concatenate([u["at"], u["rt"]], axis=0)
        sb = _wdot_nt(lhs, _bd(u["bt"], mask))
        sk = _wdot_nt(lhs, _bd(u["kt"], mask))
        u["p"] = jnp.where(strict, sb[0:C], 0.0)
        u["m_rb"] = jnp.where(incl, sb[C:2 * C], 0.0)
        u["l_ak"] = jnp.where(strict, sk[0:C], 0.0)
        u["m_rk"] = jnp.where(incl, sk[C:2 * C], 0.0)
        u["tinv"] = eye + u["p"]
        u["bt_b"] = _block_transpose(u["b_d"])
        u["bt_k"] = _block_transpose(u["k_d"])
        u["pcw"] = _block_transpose(u["pc"])
    for u in units:
        u["p"] = _wdot(u["p"], _bd(u["p"], mask))
    for j in range(1, 6):
        for u in units:
            pbd = _bd(u["p"], mask)
            if j < 5:
                both = _wdot(jnp.concatenate([u["p"], u["tinv"]], axis=0), pbd)
                u["p"] = both[0:C]
                u["tinv"] = u["tinv"] + both[C:2 * C]
            else:
                u["tinv"] = u["tinv"] + _wdot(u["tinv"], pbd)
    for u in units:
        rv = _wdot(jnp.concatenate([u["l_ak"], u["m_rk"], u["bt_k"]], axis=0), _bd(u["v"], mask))
        u["x"], u["y0"], u["psi"] = rv[0:C], rv[C:2 * C], rv[2 * C:3 * C]
    for u in units:
        u["w"] = _wdot(u["tinv"], _bd(u["at"], mask))
        u["u0"] = _wdot(u["tinv"], _bd(u["x"], mask))
    for u in units:
        lhs = jnp.concatenate([u["bt_b"], u["m_rb"]], axis=0)
        rw_ = _wdot(lhs, _bd(u["w"], mask))
        ru = _wdot(lhs, _bd(u["u0"], mask))
        u["po"] = jnp.concatenate([rw_[0:C], u["rt"] + rw_[C:2 * C]], axis=0)
        u["psi"] = u["psi"] + ru[0:C]
        u["y0"] = u["y0"] + ru[C:2 * C]

    hcur = list(hcur)
    ys = [[None] * n_groups for _ in range(n_chunks)]
    for u in units:
        c, gi = u["c"], u["g"]
        ph = _wdot(u["po"], _bd(hcur[gi], mask))
        ys[c][gi] = ph[C:2 * C] + u["y0"]
        hcur[gi] = u["pcw"] * hcur[gi] + ph[0:C] + u["psi"]
    y = jnp.concatenate([jnp.concatenate(ys[c], axis=1) for c in range(n_chunks)], axis=0)
    return y, hcur


def _rwkv_kernel(z_ref, mu_ref, w0_ref, w2_ref, a0_ref, a2_ref, g2_ref, kk_ref, ka_ref, rk_ref,
                 lng_ref, lnb_ref, mask_ref, y_ref, zlast_ref, state_ref):
    T = z_ref.shape[1]
    n_groups = RWKV_WIDTH // GROUP

    @pl.when(pl.program_id(1) == 0)
    def _():
        zlast_ref[...] = jnp.zeros_like(zlast_ref)
        state_ref[...] = jnp.zeros_like(state_ref)

    mask = mask_ref[...]
    prm = dict(mu=mu_ref[...], w0=w0_ref[...], w2=w2_ref[...], a0=a0_ref[...], a2=a2_ref[...],
               g2=g2_ref[...], k_k=kk_ref[...], k_a=ka_ref[...], r_k=rk_ref[...])
    z = z_ref[0]
    prep, lpcs = _rwkv_prep(z, zlast_ref[0:1, :], prm, mask)
    zlast_ref[0:1, :] = z[T - 1:T, :]
    y, hcur = _rwkv_solve(lambda name, c: prep[name][c], lpcs, [state_ref[gi] for gi in range(n_groups)],
                          mask, T // CHUNK)
    for gi in range(n_groups):
        state_ref[gi] = hcur[gi]
    y = _head_norm(y, mask, RWKV_LN_EPS) * lng_ref[...] + lnb_ref[...]
    y_ref[0] = (y + prep["bonus"]) * prep["g"]


def _rwkv(z, mu, w0, w2, a0, a2, g2, k_k, k_a, r_k, ln_g, ln_b, mask):
    B, S, _ = z.shape
    W = RWKV_WIDTH
    T = RWKV_TOKENS
    row = lambda b, t: (b, t, 0)
    vec = lambda a: a.reshape(1, -1)
    full = lambda a: pl.BlockSpec(a.shape, lambda b, t: (0, 0))
    args = [vec(mu), vec(w0), w2.astype(BF16), vec(a0), a2.astype(BF16), g2.astype(BF16),
            vec(k_k), vec(k_a), vec(r_k), vec(ln_g), vec(ln_b), mask]
    return pl.pallas_call(
        _rwkv_kernel,
        grid=(B, S // T),
        in_specs=[pl.BlockSpec((1, T, RWKV_COLS), row)] + [full(a) for a in args],
        out_specs=pl.BlockSpec((1, T, W), row),
        out_shape=jax.ShapeDtypeStruct((B, S, W), F32),
        scratch_shapes=[pltpu.VMEM((8, RWKV_COLS), F32),
                        pltpu.VMEM((W // GROUP, HEAD_DIM, GROUP), F32)],
        compiler_params=pltpu.CompilerParams(
            dimension_semantics=("parallel", "arbitrary"), vmem_limit_bytes=VMEM_LIMIT_BYTES),
        name="rwkv",
    )(z, *args)


def _ret_kernel(z_ref, cos_ref, sin_ref, dm_ref, qdec_ref, kdec_ref, gc_ref, gng_ref, mask_ref,
                y_ref, state_ref, ybuf_ref):
    C = CHUNK
    W = RET_WIDTH
    T = z_ref.shape[1]
    n_chunks = T // C
    n_groups = W // GROUP

    @pl.when(pl.program_id(1) == 0)
    def _():
        state_ref[...] = jnp.zeros_like(state_ref)

    z = z_ref[0]
    q = z[:, 0:W]
    k = z[:, W:2 * W]
    v = z[:, 2 * W:3 * W]
    g = z[:, 3 * W:4 * W]
    reps = W // cos_ref.shape[1]
    cosf = jnp.concatenate([cos_ref[...]] * reps, axis=1)
    sinf = jnp.concatenate([sin_ref[...]] * reps, axis=1)
    lane = lax.broadcasted_iota(jnp.int32, (1, W), 1)
    first_half = (lane % HEAD_DIM) < (HEAD_DIM // 2)

    def rot(t):
        swapped = jnp.where(first_half, pltpu.roll(t, W - HEAD_DIM // 2, 1), pltpu.roll(t, HEAD_DIM // 2, 1))
        return t * cosf + swapped * sinf

    q = rot(q) * (HEAD_DIM ** -0.5)
    k = rot(k)
    mask = mask_ref[...]

    units = []
    for c in range(n_chunks):
        rows = slice(c * C, (c + 1) * C)
        qd = q[rows] * qdec_ref[...]
        kd = k[rows] * kdec_ref[...]
        for gi in range(n_groups):
            ln = slice(gi * GROUP, (gi + 1) * GROUP)
            units.append(dict(c=c, g=gi, q=q[rows, ln], k=k[rows, ln], v=v[rows, ln], qd=qd[:, ln], kd=kd[:, ln]))
    for u in units:
        u["sc"] = _wdot_nt(u["q"], _bd(u["k"], mask)) * dm_ref[:, u["g"] * GROUP:(u["g"] + 1) * GROUP]
        u["kdt"] = _block_transpose(u["kd"])
    for u in units:
        both = _wdot(jnp.concatenate([u["sc"], u["kdt"]], axis=0), _bd(u["v"], mask))
        u["y"], u["kv"] = both[0:C], both[C:2 * C]

    scur = [state_ref[gi] for gi in range(n_groups)]
    for u in units:
        c, gi = u["c"], u["g"]
        ln = slice(gi * GROUP, (gi + 1) * GROUP)
        ybuf_ref[c * C:(c + 1) * C, ln] = u["y"] + _wdot(u["qd"], _bd(scur[gi], mask))
        scur[gi] = scur[gi] * gc_ref[:, ln] + u["kv"]
    for gi in range(n_groups):
        state_ref[gi] = scur[gi]

    y = _head_norm(ybuf_ref[...], mask, RET_GN_EPS) * gng_ref[...]
    y_ref[0] = g * _sigmoid(g) * y


def _ret(z, cos2, sin2, dmw, qdec, kdec, gcw, gn_g, mask):
    B, S, _ = z.shape
    W = RET_WIDTH
    T = RET_TOKENS
    row = lambda b, t: (b, t, 0)
    args = [dmw, qdec, kdec, gcw, gn_g.reshape(1, W), mask]
    full = lambda a: pl.BlockSpec(a.shape, lambda b, t: (0,) * a.ndim)
    return pl.pallas_call(
        _ret_kernel,
        grid=(B, S // T),
        in_specs=[pl.BlockSpec((1, T, RET_COLS), row),
                  pl.BlockSpec((T, cos2.shape[1]), lambda b, t: (t, 0)),
                  pl.BlockSpec((T, sin2.shape[1]), lambda b, t: (t, 0))] + [full(a) for a in args],
        out_specs=pl.BlockSpec((1, T, W), row),
        out_shape=jax.ShapeDtypeStruct((B, S, W), F32),
        scratch_shapes=[pltpu.VMEM((W // GROUP, HEAD_DIM, GROUP), F32),
                        pltpu.VMEM((T, W), F32)],
        compiler_params=pltpu.CompilerParams(
            dimension_semantics=("parallel", "arbitrary"), vmem_limit_bytes=VMEM_LIMIT_BYTES),
        name="ret",
    )(z, cos2, sin2, *args)


def _ffn_kernel(x_ref, yr_ref, yt_ref, gta_ref, shf_ref, scf_ref, gtf_ref, wout_ref, gnf_ref,
                wup_ref, cw_ref, cb_ref, wdn_ref, gfin_ref, o_ref, carry_ref, ubuf_ref, abuf_ref):
    tm = x_ref.shape[1]

    @pl.when(pl.program_id(1) == 0)
    def _():
        carry_ref[...] = jnp.zeros_like(carry_ref)

    ymix = (jnp.dot(yr_ref[0].astype(BF16), wout_ref[0:RWKV_WIDTH, :], preferred_element_type=F32)
            + jnp.dot(yt_ref[0].astype(BF16), wout_ref[RWKV_WIDTH:, :], preferred_element_type=F32))
    x1 = x_ref[0] + gta_ref[0] * ymix
    h = (_rms(x1) * gnf_ref[...] * (1.0 + scf_ref[0]) + shf_ref[0]).astype(BF16)
    row8 = lax.broadcasted_iota(jnp.int32, (8, 1), 0)

    def conv(u, col):
        tf = u.shape[1]
        prev = carry_ref[:, col:col + tf]
        r1, r2 = pltpu.roll(u, 1, 0), pltpu.roll(u, 2, 0)
        h1 = jnp.where(row8 == 0, prev[7:8, :], r1[0:8])
        h2 = jnp.where(row8 == 0, prev[6:7, :], jnp.where(row8 == 1, prev[7:8, :], r2[0:8]))
        u1 = jnp.concatenate([h1, r1[8:]], axis=0)
        u2 = jnp.concatenate([h2, r2[8:]], axis=0)
        carry_ref[:, col:col + tf] = u[tm - 8:tm, :]
        cw = cw_ref[:, col:col + tf]
        return cb_ref[:, col:col + tf] + u2 * cw[0:1, :] + u1 * cw[1:2, :] + u * cw[2:3, :]

    tf = FFN_COLS
    tiles = list(range(0, D_FF, tf))

    def up(n):
        j, slot = tiles[n], n % 2
        ubuf_ref[slot, :, 0:tf] = jnp.dot(h, wup_ref[:, j:j + tf], preferred_element_type=F32)
        ubuf_ref[slot, :, tf:2 * tf] = jnp.dot(h, wup_ref[:, D_FF + j:D_FF + j + tf], preferred_element_type=F32)

    def glu(n):
        j, slot = tiles[n], n % 2
        val = conv(ubuf_ref[slot, :, 0:tf], j)
        gate = conv(ubuf_ref[slot, :, tf:2 * tf], D_FF + j)
        abuf_ref[:, j:j + tf] = (gate * _sigmoid(gate) * val).astype(BF16)

    up(0)
    for n in range(1, len(tiles)):
        up(n)
        glu(n - 1)
    glu(len(tiles) - 1)
    y = jnp.dot(abuf_ref[...], wdn_ref[...], preferred_element_type=F32)
    x2 = x1 + gtf_ref[0] * y
    o_ref[0] = _rms(x2) * gfin_ref[...]


def _ffn(x, y_rwkv, y_ret, gt_a, sh_f, sc_f, gt_f, w_out, g_ffn, w_up, conv_w, conv_b, w_down, g_fin):
    B, S, D = x.shape
    tm = FFN_ROWS
    row = lambda b, t: (b, t, 0)
    per_b = lambda b, t: (b, 0, 0)
    const = lambda b, t: (0, 0)
    once = lambda a: pl.BlockSpec(a.shape, const, pipeline_mode=pl.Buffered(1))
    vecs = [g_ffn.reshape(1, D)]
    return pl.pallas_call(
        _ffn_kernel,
        grid=(B, S // tm),
        in_specs=[pl.BlockSpec((1, tm, D), row),
                  pl.BlockSpec((1, tm, RWKV_WIDTH), row),
                  pl.BlockSpec((1, tm, RET_WIDTH), row),
                  pl.BlockSpec((1, 1, D), per_b), pl.BlockSpec((1, 1, D), per_b),
                  pl.BlockSpec((1, 1, D), per_b), pl.BlockSpec((1, 1, D), per_b),
                  once(w_out), once(vecs[0]), once(w_up), once(conv_w), pl.BlockSpec((1, 2 * D_FF), const),
                  once(w_down), pl.BlockSpec((1, D), const)],
        out_specs=pl.BlockSpec((1, tm, D), row),
        out_shape=jax.ShapeDtypeStruct((B, S, D), F32),
        scratch_shapes=[pltpu.VMEM((8, 2 * D_FF), F32),
                        pltpu.VMEM((2, tm, 2 * FFN_COLS), F32),
                        pltpu.VMEM((tm, D_FF), BF16)],
        compiler_params=pltpu.CompilerParams(
            dimension_semantics=("parallel", "arbitrary"), vmem_limit_bytes=VMEM_LIMIT_BYTES),
        name="ffn",
    )(x, y_rwkv, y_ret, gt_a, sh_f, sc_f, gt_f, w_out, vecs[0], w_up, conv_w,
      conv_b.reshape(1, 2 * D_FF), w_down, g_fin.reshape(1, D))


def _tables(S):
    pos = jnp.arange(S, dtype=F32)
    inv_freq = ROPE_BASE ** (-jnp.arange(0, HEAD_DIM, 2, dtype=F32) / HEAD_DIM)
    ang = pos[:, None] * inv_freq[None, :]
    cos, sin = jnp.cos(ang), jnp.sin(ang)
    cos2 = jnp.concatenate([cos, cos, cos, cos], axis=1)
    sin2 = jnp.concatenate([-sin, sin, -sin, sin], axis=1)
    log_gamma = jnp.log1p(-(2.0 ** (-5.0 - jnp.arange(N_HEADS, dtype=F32))))
    idx = jnp.arange(CHUNK, dtype=F32)
    dmat = jnp.exp(log_gamma[:, None, None] * jnp.abs(idx[:, None] - idx[None, :]))
    q_dec = jnp.exp(log_gamma[:, None] * (idx + 1.0))
    k_dec = jnp.exp(log_gamma[:, None] * (CHUNK - 1.0 - idx))
    widen = lambda t: jnp.repeat(t.T, HEAD_DIM, axis=1)
    dmw = dmat.transpose(1, 0, 2).reshape(CHUNK, N_HEADS * CHUNK)
    gcw = jnp.repeat(jnp.exp(log_gamma * CHUNK), HEAD_DIM)[None, :]
    head = jnp.arange(GROUP) // HEAD_DIM
    mask = (head[:, None] == head[None, :]).astype(BF16)
    return cos2, sin2, dmw, widen(q_dec), widen(k_dec), gcw, mask


def kernel(x, c, w_ada, b_ada, attn_norm_g, w_in, rwkv_mu, rwkv_w0, rwkv_w2, rwkv_a0, rwkv_a2, rwkv_g2,
           rwkv_k_k, rwkv_k_a, rwkv_r_k, rwkv_ln_g, rwkv_ln_b, ret_gn_g, w_out, ffn_norm_g, ffn_w_up,
           ffn_conv_w, ffn_conv_b, ffn_w_down, final_norm_g):
    B, S, D = x.shape
    assert D == D_MODEL and S % FFN_ROWS == 0 and w_ada.shape[0] == 1
    cos2, sin2, dmw, qdec, kdec, gcw, mask = _tables(S)
    mod = _ada(c, w_ada[0], b_ada[0])
    sh_a, sc_a, gt_a, sh_f, sc_f, gt_f = [m.reshape(B, 1, D) for m in jnp.split(mod, N_MOD, axis=-1)]
    z_rwkv, z_ret = _inproj(x, attn_norm_g[0], sc_a, sh_a, w_in[0].astype(BF16))
    y_rwkv = _rwkv(z_rwkv, rwkv_mu[0], rwkv_w0[0], rwkv_w2[0], rwkv_a0[0], rwkv_a2[0], rwkv_g2[0],
                   rwkv_k_k[0], rwkv_k_a[0], rwkv_r_k[0], rwkv_ln_g[0], rwkv_ln_b[0], mask)
    y_ret = _ret(z_ret, cos2, sin2, dmw, qdec, kdec, gcw, ret_gn_g[0], mask)
    return _ffn(x, y_rwkv, y_ret, gt_a, sh_f, sc_f, gt_f, w_out[0].astype(BF16), ffn_norm_g[0],
                ffn_w_up[0].astype(BF16), ffn_conv_w[0], ffn_conv_b[0], ffn_w_down[0].astype(BF16),
                final_norm_g)
```

```python
import functools
import math

import jax
import jax.numpy as jnp
from jax import lax
from jax.experimental import pallas as pl
from jax.experimental.pallas import tpu as pltpu

F32 = jnp.float32
BF16 = jnp.bfloat16

D_MODEL = 1024
CHUNK = 64
HEAD_DIM = 64
RWKV_WIDTH = 512
RET_WIDTH = 512
N_HEADS = 8
DECAY_LORA = 64
AAA_LORA = 64
GATE_LORA = 128
RWKV_COLS = 3 * RWKV_WIDTH + DECAY_LORA + AAA_LORA + GATE_LORA
RET_COLS = 4 * RET_WIDTH
D_FF = 2816
ROPE_BASE = 10000.0
NORM_EPS = 1e-6
RWKV_LN_EPS = 64e-5
RET_GN_EPS = 1e-6
W_DECAY_SCALE = math.exp(-0.5)
N_MOD = 6
HEADS_PER_GROUP = 4
GROUP = HEADS_PER_GROUP * HEAD_DIM

VMEM_LIMIT_BYTES = 56 * 1024 * 1024
MIXER_TOKENS = 512
FFN_ROWS = 512
FFN_COLS = 256
ADA_COLS = 1536

_NT = (((1,), (1,)), ((), ()))


def _bdot(a, b):
    return jnp.dot(a.astype(BF16), b.astype(BF16), preferred_element_type=F32)


def _split_parts(x, parts=3):
    out, rem = [], x
    for i in range(parts):
        p = rem.astype(BF16)
        out.append(p)
        if i + 1 < parts:
            rem = rem - p.astype(F32)
    return out


def _head_sums(x, mask, parts=2):
    outs = []
    for g in range(x.shape[1] // GROUP):
        acc = None
        for p in _split_parts(x[:, g * GROUP:(g + 1) * GROUP], parts=parts):
            d = jnp.dot(p, mask, preferred_element_type=F32)
            acc = d if acc is None else acc + d
        outs.append(acc)
    return jnp.concatenate(outs, axis=1)


def _exact_dot_left(m_bf16, x):
    acc = None
    for p in _split_parts(x):
        d = jnp.dot(m_bf16, p, preferred_element_type=F32)
        acc = d if acc is None else acc + d
    return acc


def _sigmoid(x):
    return 1.0 / (1.0 + jnp.exp(-x))


def _rms(x):
    return x * lax.rsqrt(jnp.mean(x * x, axis=-1, keepdims=True) + NORM_EPS)


def _ada_kernel(c_ref, w_ref, b_ref, o_ref):
    cv = c_ref[...]
    s = cv * _sigmoid(cv)
    o_ref[...] = jnp.dot(s, w_ref[...], preferred_element_type=F32,
                         precision=lax.Precision.HIGHEST) + b_ref[...]


def _ada(c, w, b):
    B = c.shape[0]
    n = w.shape[1]
    return pl.pallas_call(
        _ada_kernel,
        grid=(n // ADA_COLS,),
        in_specs=[pl.BlockSpec((B, D_MODEL), lambda j: (0, 0)),
                  pl.BlockSpec((D_MODEL, ADA_COLS), lambda j: (0, j)),
                  pl.BlockSpec((1, ADA_COLS), lambda j: (0, j))],
        out_specs=pl.BlockSpec((B, ADA_COLS), lambda j: (0, j)),
        out_shape=jax.ShapeDtypeStruct((B, n), F32),
        compiler_params=pltpu.CompilerParams(vmem_limit_bytes=VMEM_LIMIT_BYTES),
        name="ada",
    )(c, w, b.reshape(1, n))


def _head_norm(y, mask, eps):
    mean = _head_sums(y, mask) * (1.0 / HEAD_DIM)
    d = y - mean
    var = _head_sums(d * d, mask, parts=1) * (1.0 / HEAD_DIM)
    return d * lax.rsqrt(var + eps)


def _shift_rows(x, first):
    rolled = pltpu.roll(x, 1, 0)
    head = jnp.where(lax.broadcasted_iota(jnp.int32, (8, 1), 0) == 0, first, rolled[0:8])
    return jnp.concatenate([head, rolled[8:]], axis=0)


def _bd(x, mask):
    xb = x.astype(BF16)
    half = GROUP // 2
    zeros = jnp.zeros((HEAD_DIM, half), BF16)
    blocks = []
    for h in range(HEADS_PER_GROUP):
        rows = slice(h * HEAD_DIM, (h + 1) * HEAD_DIM)
        if h < HEADS_PER_GROUP // 2:
            blocks.append(jnp.concatenate([xb[:, 0:half] * mask[rows, 0:half], zeros], axis=1))
        else:
            blocks.append(jnp.concatenate([zeros, xb[:, half:GROUP] * mask[rows, half:GROUP]], axis=1))
    return jnp.concatenate(blocks, axis=0)


def _block_transpose(x):
    r = jnp.concatenate([x] * HEADS_PER_GROUP, axis=0).T
    blk = lax.broadcasted_iota(jnp.int32, (1, GROUP), 1) // HEAD_DIM
    out = r[0:HEAD_DIM]
    for h in range(1, HEADS_PER_GROUP):
        out = jnp.where(blk == h, r[h * HEAD_DIM:(h + 1) * HEAD_DIM], out)
    return out


def _wdot(lhs, rhs_bd):
    return jnp.dot(lhs.astype(BF16), rhs_bd, preferred_element_type=F32)


def _wdot_nt(lhs, rhs_bd):
    return lax.dot_general(lhs.astype(BF16), rhs_bd, _NT, preferred_element_type=F32)


def _interleave(*gens):
    results = [None] * len(gens)
    live = list(range(len(gens)))
    while live:
        for i in list(live):
            try:
                next(gens[i])
            except StopIteration as stop:
                results[i] = stop.value
                live.remove(i)
    return results


_CHUNK_FIELDS = ("at", "rt", "bt", "kt", "v", "b_d", "k_d")


def _rwkv_prep(z, zfirst, prm, mask):
    C, W = CHUNK, RWKV_WIDTH
    T = z.shape[0]
    zprev = _shift_rows(z, zfirst)
    zm = z + prm["mu"] * (zprev - z)
    r = zm[:, 0:W]
    k = zm[:, W:2 * W]
    v = zm[:, 2 * W:3 * W]
    wd = zm[:, 3 * W:3 * W + DECAY_LORA]
    ad = zm[:, 3 * W + DECAY_LORA:3 * W + DECAY_LORA + AAA_LORA]
    gd = zm[:, 3 * W + DECAY_LORA + AAA_LORA:RWKV_COLS]

    lw = -W_DECAY_SCALE * _sigmoid(prm["w0"] + _bdot(jnp.tanh(wd), prm["w2"]))
    a = _sigmoid(prm["a0"] + _bdot(ad, prm["a2"]))
    g = _bdot(_sigmoid(gd), prm["g2"])
    kkr = k * prm["k_k"]
    yield
    kkn = kkr * lax.rsqrt(jnp.maximum(_head_sums(kkr * kkr, mask, parts=1), 1e-24))
    km = k * (1.0 + (a - 1.0) * prm["k_a"])
    kb = kkn * a
    yield

    ri = lax.broadcasted_iota(jnp.int32, (C, C), 0)
    ci = lax.broadcasted_iota(jnp.int32, (C, C), 1)
    tri = (ri >= ci).astype(BF16)
    out = {name: [] for name in _CHUNK_FIELDS}
    lpcs = []
    for c in range(T // C):
        rows = slice(c * C, (c + 1) * C)
        lw_c = lw[rows]
        lp = _exact_dot_left(tri, lw_c)
        em = jnp.exp(-lp)
        lpc = lp[C - 1:C, :]
        ee = jnp.exp(lpc - lp)
        out["rt"].append(r[rows] * jnp.exp(lp))
        out["at"].append(-kkn[rows] * jnp.exp(lp - lw_c))
        out["bt"].append(kb[rows] * em)
        out["kt"].append(km[rows] * em)
        out["b_d"].append(kb[rows] * ee)
        out["k_d"].append(km[rows] * ee)
        out["v"].append(v[rows])
        lpcs.append(lpc)
        yield
    out["bonus"] = _head_sums(r * km * prm["r_k"], mask) * v
    out["g"] = g
    return out, lpcs


def _rwkv_solve(get, lpcs, hcur, mask, n_chunks):
    C = CHUNK
    n_groups = RWKV_WIDTH // GROUP
    rw = lax.broadcasted_iota(jnp.int32, (C, GROUP), 0)
    cw = lax.broadcasted_iota(jnp.int32, (C, GROUP), 1) % HEAD_DIM
    strict = rw > cw
    incl = rw >= cw
    eye = (rw == cw).astype(F32)

    units = []
    for c in range(n_chunks):
        fields = {name: get(name, c) for name in _CHUNK_FIELDS}
        pc = jnp.broadcast_to(jnp.exp(lpcs[c]), (C, RWKV_WIDTH))
        for gi in range(n_groups):
            ln = slice(gi * GROUP, (gi + 1) * GROUP)
            u = {name: val[:, ln] for name, val in fields.items()}
            u.update(c=c, g=gi, pc=pc[:, ln])
            units.append(u)
    for u in units:
        lhs = jnp.concatenate([u["at"], u["rt"]], axis=0)
        sb = _wdot_nt(lhs, _bd(u["bt"], mask))
        sk = _wdot_nt(lhs, _bd(u["kt"], mask))
        u["p"] = jnp.where(strict, sb[0:C], 0.0)
        u["m_rb"] = jnp.where(incl, sb[C:2 * C], 0.0)
        u["l_ak"] = jnp.where(strict, sk[0:C], 0.0)
        u["m_rk"] = jnp.where(incl, sk[C:2 * C], 0.0)
        u["tinv"] = eye + u["p"]
        u["bt_b"] = _block_transpose(u["b_d"])
        u["bt_k"] = _block_transpose(u["k_d"])
        u["pcw"] = _block_transpose(u["pc"])
    yield
    for u in units:
        u["p"] = _wdot(u["p"], _bd(u["p"], mask))
    for j in range(1, 6):
        yield
        for u in units:
            pbd = _bd(u["p"], mask)
            if j < 5:
                both = _wdot(jnp.concatenate([u["p"], u["tinv"]], axis=0), pbd)
                u["p"] = both[0:C]
                u["tinv"] = u["tinv"] + both[C:2 * C]
            else:
                u["tinv"] = u["tinv"] + _wdot(u["tinv"], pbd)
    yield
    for u in units:
        rv = _wdot(jnp.concatenate([u["l_ak"], u["m_rk"], u["bt_k"]], axis=0), _bd(u["v"], mask))
        u["x"], u["y0"], u["psi"] = rv[0:C], rv[C:2 * C], rv[2 * C:3 * C]
    yield
    for u in units:
        u["w"] = _wdot(u["tinv"], _bd(u["at"], mask))
        u["u0"] = _wdot(u["tinv"], _bd(u["x"], mask))
    yield
    for u in units:
        lhs = jnp.concatenate([u["bt_b"], u["m_rb"]], axis=0)
        rw_ = _wdot(lhs, _bd(u["w"], mask))
        ru = _wdot(lhs, _bd(u["u0"], mask))
        u["po"] = jnp.concatenate([rw_[0:C], u["rt"] + rw_[C:2 * C]], axis=0)
        u["psi"] = u["psi"] + ru[0:C]
        u["y0"] = u["y0"] + ru[C:2 * C]
    yield

    hcur = list(hcur)
    ys = [[None] * n_groups for _ in range(n_chunks)]
    for u in units:
        c, gi = u["c"], u["g"]
        ph = _wdot(u["po"], _bd(hcur[gi], mask))
        ys[c][gi] = ph[C:2 * C] + u["y0"]
        hcur[gi] = u["pcw"] * hcur[gi] + ph[0:C] + u["psi"]
    y = jnp.concatenate([jnp.concatenate(ys[c], axis=1) for c in range(n_chunks)], axis=0)
    return y, hcur


def _ret_mix(z_ref, cos_ref, sin_ref, dm_ref, qdec_ref, kdec_ref, gc_ref, scur, mask):
    C = CHUNK
    W = RET_WIDTH
    T = z_ref.shape[0]
    n_chunks = T // C
    n_groups = W // GROUP
    reps = W // cos_ref.shape[1]
    cosf = jnp.concatenate([cos_ref[...]] * reps, axis=1)
    sinf = jnp.concatenate([sin_ref[...]] * reps, axis=1)
    lane = lax.broadcasted_iota(jnp.int32, (1, W), 1)
    first_half = (lane % HEAD_DIM) < (HEAD_DIM // 2)

    def rot(t):
        swapped = jnp.where(first_half, pltpu.roll(t, W - HEAD_DIM // 2, 1), pltpu.roll(t, HEAD_DIM // 2, 1))
        return t * cosf + swapped * sinf

    q = rot(z_ref[:, 0:W]) * (HEAD_DIM ** -0.5)
    yield
    k = rot(z_ref[:, W:2 * W])
    v = z_ref[:, 2 * W:3 * W]
    yield

    units = []
    for c in range(n_chunks):
        rows = slice(c * C, (c + 1) * C)
        qd = q[rows] * qdec_ref[...]
        kd = k[rows] * kdec_ref[...]
        for gi in range(n_groups):
            ln = slice(gi * GROUP, (gi + 1) * GROUP)
            units.append(dict(c=c, g=gi, q=q[rows, ln], k=k[rows, ln], v=v[rows, ln], qd=qd[:, ln], kd=kd[:, ln]))
    for u in units:
        u["sc"] = _wdot_nt(u["q"], _bd(u["k"], mask)) * dm_ref[:, u["g"] * GROUP:(u["g"] + 1) * GROUP]
        u["kdt"] = _block_transpose(u["kd"])
    yield
    for u in units:
        both = _wdot(jnp.concatenate([u["sc"], u["kdt"]], axis=0), _bd(u["v"], mask))
        u["y"], u["kv"] = both[0:C], both[C:2 * C]
    yield

    scur = list(scur)
    ys = [[None] * n_groups for _ in range(n_chunks)]
    for u in units:
        c, gi = u["c"], u["g"]
        ys[c][gi] = u["y"] + _wdot(u["qd"], _bd(scur[gi], mask))
        scur[gi] = scur[gi] * gc_ref[:, gi * GROUP:(gi + 1) * GROUP] + u["kv"]
    y = jnp.concatenate([jnp.concatenate(ys[c], axis=1) for c in range(n_chunks)], axis=0)
    return y, scur


def _mixer_kernel(x_ref, g_ref, sc_ref, sh_ref, win_ref,
                  mu_ref, w0_ref, w2_ref, a0_ref, a2_ref, g2_ref, kk_ref, ka_ref, rk_ref, lng_ref, lnb_ref,
                  cos_ref, sin_ref, dm_ref, qdec_ref, kdec_ref, gc_ref, gng_ref, mask_ref,
                  yr_ref, yt_ref, zr_ref, zt_ref, zlast_ref, rstate_ref, tstate_ref):
    T = x_ref.shape[1]
    n_groups = RWKV_WIDTH // GROUP

    @pl.when(pl.program_id(1) == 0)
    def _():
        zlast_ref[...] = jnp.zeros_like(zlast_ref)
        rstate_ref[...] = jnp.zeros_like(rstate_ref)
        tstate_ref[...] = jnp.zeros_like(tstate_ref)

    h = (_rms(x_ref[0]) * g_ref[...] * (1.0 + sc_ref[0]) + sh_ref[0]).astype(BF16)
    step = 256
    for j in range(0, RWKV_COLS, step):
        zr_ref[:, j:j + step] = jnp.dot(h, win_ref[:, j:j + step], preferred_element_type=F32)

    def ret_proj():
        for j in range(0, RET_COLS, step):
            zt_ref[:, j:j + step] = jnp.dot(h, win_ref[:, RWKV_COLS + j:RWKV_COLS + j + step],
                                            preferred_element_type=F32)
            yield

    mask = mask_ref[...]
    prm = dict(mu=mu_ref[...], w0=w0_ref[...], w2=w2_ref[...], a0=a0_ref[...], a2=a2_ref[...],
               g2=g2_ref[...], k_k=kk_ref[...], k_a=ka_ref[...], r_k=rk_ref[...])
    (prep, lpcs), _ = _interleave(_rwkv_prep(zr_ref[...], zlast_ref[0:1, :], prm, mask), ret_proj())
    zlast_ref[0:1, :] = zr_ref[T - 1:T, :]

    (y, hcur), (yt, scur) = _interleave(
        _rwkv_solve(lambda name, c: prep[name][c], lpcs, [rstate_ref[gi] for gi in range(n_groups)],
                    mask, T // CHUNK),
        _ret_mix(zt_ref, cos_ref, sin_ref, dm_ref, qdec_ref, kdec_ref, gc_ref,
                 [tstate_ref[gi] for gi in range(n_groups)], mask))
    for gi in range(n_groups):
        rstate_ref[gi] = hcur[gi]
        tstate_ref[gi] = scur[gi]
    y = _head_norm(y, mask, RWKV_LN_EPS) * lng_ref[...] + lnb_ref[...]
    yr_ref[0] = (y + prep["bonus"]) * prep["g"]
    gate = zt_ref[:, 3 * RET_WIDTH:4 * RET_WIDTH]
    yt_ref[0] = gate * _sigmoid(gate) * (_head_norm(yt, mask, RET_GN_EPS) * gng_ref[...])


def _mixer(x, g, sc, sh, w_in, mu, w0, w2, a0, a2, g2, k_k, k_a, r_k, ln_g, ln_b,
           cos2, sin2, dmw, qdec, kdec, gcw, gn_g, mask):
    B, S, D = x.shape
    T = MIXER_TOKENS
    W = RWKV_WIDTH
    row = lambda b, t: (b, t, 0)
    per_b = lambda b, t: (b, 0, 0)
    vec = lambda a: a.reshape(1, -1)
    full = lambda a: pl.BlockSpec(a.shape, lambda b, t: (0, 0))
    tab = lambda a: pl.BlockSpec((T, a.shape[1]), lambda b, t: (t, 0))
    params = [vec(mu), vec(w0), w2.astype(BF16), vec(a0), a2.astype(BF16), g2.astype(BF16),
              vec(k_k), vec(k_a), vec(r_k), vec(ln_g), vec(ln_b)]
    tables = [dmw, qdec, kdec, gcw, vec(gn_g), mask]
    return pl.pallas_call(
        _mixer_kernel,
        grid=(B, S // T),
        in_specs=[pl.BlockSpec((1, T, D), row), full(vec(g)), pl.BlockSpec((1, 1, D), per_b),
                  pl.BlockSpec((1, 1, D), per_b), full(w_in)]
                 + [full(a) for a in params] + [tab(cos2), tab(sin2)] + [full(a) for a in tables],
        out_specs=[pl.BlockSpec((1, T, W), row), pl.BlockSpec((1, T, RET_WIDTH), row)],
        out_shape=[jax.ShapeDtypeStruct((B, S, W), F32), jax.ShapeDtypeStruct((B, S, RET_WIDTH), F32)],
        scratch_shapes=[pltpu.VMEM((T, RWKV_COLS), F32),
                        pltpu.VMEM((T, RET_COLS), F32),
                        pltpu.VMEM((8, RWKV_COLS), F32),
                        pltpu.VMEM((W // GROUP, HEAD_DIM, GROUP), F32),
                        pltpu.VMEM((RET_WIDTH // GROUP, HEAD_DIM, GROUP), F32)],
        compiler_params=pltpu.CompilerParams(
            dimension_semantics=("parallel", "arbitrary"), vmem_limit_bytes=VMEM_LIMIT_BYTES),
        name="mixer",
    )(x, vec(g), sc, sh, w_in, *params, cos2, sin2, *tables)


def _ffn_kernel(x_ref, yr_ref, yt_ref, gta_ref, shf_ref, scf_ref, gtf_ref, wout_ref, gnf_ref,
                wup_ref, cw_ref, cb_ref, wdn_ref, gfin_ref, o_ref, carry_ref, ubuf_ref, abuf_ref):
    tm = x_ref.shape[1]

    @pl.when(pl.program_id(1) == 0)
    def _():
        carry_ref[...] = jnp.zeros_like(carry_ref)

    ymix = (jnp.dot(yr_ref[0].astype(BF16), wout_ref[0:RWKV_WIDTH, :], preferred_element_type=F32)
            + jnp.dot(yt_ref[0].astype(BF16), wout_ref[RWKV_WIDTH:, :], preferred_element_type=F32))
    x1 = x_ref[0] + gta_ref[0] * ymix
    h = (_rms(x1) * gnf_ref[...] * (1.0 + scf_ref[0]) + shf_ref[0]).astype(BF16)
    row8 = lax.broadcasted_iota(jnp.int32, (8, 1), 0)

    def conv(u, col):
        tf = u.shape[1]
        prev = carry_ref[:, col:col + tf]
        r1, r2 = pltpu.roll(u, 1, 0), pltpu.roll(u, 2, 0)
        h1 = jnp.where(row8 == 0, prev[7:8, :], r1[0:8])
        h2 = jnp.where(row8 == 0, prev[6:7, :], jnp.where(row8 == 1, prev[7:8, :], r2[0:8]))
        u1 = jnp.concatenate([h1, r1[8:]], axis=0)
        u2 = jnp.concatenate([h2, r2[8:]], axis=0)
        carry_ref[:, col:col + tf] = u[tm - 8:tm, :]
        cw = cw_ref[:, col:col + tf]
        return cb_ref[:, col:col + tf] + u2 * cw[0:1, :] + u1 * cw[1:2, :] + u * cw[2:3, :]

    tf = FFN_COLS
    tiles = list(range(0, D_FF, tf))

    def up(n):
        j, slot = tiles[n], n % 2
        ubuf_ref[slot, :, 0:tf] = jnp.dot(h, wup_ref[:, j:j + tf], preferred_element_type=F32)
        ubuf_ref[slot, :, tf:2 * tf] = jnp.dot(h, wup_ref[:, D_FF + j:D_FF + j + tf], preferred_element_type=F32)

    def glu(n):
        j, slot = tiles[n], n % 2
        val = conv(ubuf_ref[slot, :, 0:tf], j)
        gate = conv(ubuf_ref[slot, :, tf:2 * tf], D_FF + j)
        abuf_ref[:, j:j + tf] = (gate * _sigmoid(gate) * val).astype(BF16)

    up(0)
    for n in range(1, len(tiles)):
        up(n)
        glu(n - 1)
    glu(len(tiles) - 1)
    y = jnp.dot(abuf_ref[...], wdn_ref[...], preferred_element_type=F32)
    x2 = x1 + gtf_ref[0] * y
    o_ref[0] = _rms(x2) * gfin_ref[...]


def _ffn(x, y_rwkv, y_ret, gt_a, sh_f, sc_f, gt_f, w_out, g_ffn, w_up, conv_w, conv_b, w_down, g_fin):
    B, S, D = x.shape
    tm = FFN_ROWS
    row = lambda b, t: (b, t, 0)
    per_b = lambda b, t: (b, 0, 0)
    const = lambda b, t: (0, 0)
    once = lambda a: pl.BlockSpec(a.shape, const, pipeline_mode=pl.Buffered(1))
    vecs = [g_ffn.reshape(1, D)]
    return pl.pallas_call(
        _ffn_kernel,
        grid=(B, S // tm),
        in_specs=[pl.BlockSpec((1, tm, D), row),
                  pl.BlockSpec((1, tm, RWKV_WIDTH), row),
                  pl.BlockSpec((1, tm, RET_WIDTH), row),
                  pl.BlockSpec((1, 1, D), per_b), pl.BlockSpec((1, 1, D), per_b),
                  pl.BlockSpec((1, 1, D), per_b), pl.BlockSpec((1, 1, D), per_b),
                  once(w_out), once(vecs[0]), once(w_up), once(conv_w), pl.BlockSpec((1, 2 * D_FF), const),
                  once(w_down), pl.BlockSpec((1, D), const)],
        out_specs=pl.BlockSpec((1, tm, D), row),
        out_shape=jax.ShapeDtypeStruct((B, S, D), F32),
        scratch_shapes=[pltpu.VMEM((8, 2 * D_FF), F32),
                        pltpu.VMEM((2, tm, 2 * FFN_COLS), F32),
                        pltpu.VMEM((tm, D_FF), BF16)],
        compiler_params=pltpu.CompilerParams(
            dimension_semantics=("parallel", "arbitrary"), vmem_limit_bytes=VMEM_LIMIT_BYTES),
        name="ffn",
    )(x, y_rwkv, y_ret, gt_a, sh_f, sc_f, gt_f, w_out, vecs[0], w_up, conv_w,
      conv_b.reshape(1, 2 * D_FF), w_down, g_fin.reshape(1, D))


def _tables(S):
    pos = jnp.arange(S, dtype=F32)
    inv_freq = ROPE_BASE ** (-jnp.arange(0, HEAD_DIM, 2, dtype=F32) / HEAD_DIM)
    ang = pos[:, None] * inv_freq[None, :]
    cos, sin = jnp.cos(ang), jnp.sin(ang)
    cos2 = jnp.concatenate([cos, cos, cos, cos], axis=1)
    sin2 = jnp.concatenate([-sin, sin, -sin, sin], axis=1)
    log_gamma = jnp.log1p(-(2.0 ** (-5.0 - jnp.arange(N_HEADS, dtype=F32))))
    idx = jnp.arange(CHUNK, dtype=F32)
    dmat = jnp.exp(log_gamma[:, None, None] * jnp.abs(idx[:, None] - idx[None, :]))
    q_dec = jnp.exp(log_gamma[:, None] * (idx + 1.0))
    k_dec = jnp.exp(log_gamma[:, None] * (CHUNK - 1.0 - idx))
    widen = lambda t: jnp.repeat(t.T, HEAD_DIM, axis=1)
    dmw = dmat.transpose(1, 0, 2).reshape(CHUNK, N_HEADS * CHUNK)
    gcw = jnp.repeat(jnp.exp(log_gamma * CHUNK), HEAD_DIM)[None, :]
    head = jnp.arange(GROUP) // HEAD_DIM
    mask = (head[:, None] == head[None, :]).astype(BF16)
    return cos2, sin2, dmw, widen(q_dec), widen(k_dec), gcw, mask


def kernel(x, c, w_ada, b_ada, attn_norm_g, w_in, rwkv_mu, rwkv_w0, rwkv_w2, rwkv_a0, rwkv_a2, rwkv_g2,
           rwkv_k_k, rwkv_k_a, rwkv_r_k, rwkv_ln_g, rwkv_ln_b, ret_gn_g, w_out, ffn_norm_g, ffn_w_up,
           ffn_conv_w, ffn_conv_b, ffn_w_down, final_norm_g):
    B, S, D = x.shape
    assert D == D_MODEL and S % MIXER_TOKENS == 0 and S % FFN_ROWS == 0 and w_ada.shape[0] == 1
    cos2, sin2, dmw, qdec, kdec, gcw, mask = _tables(S)
    mod = _ada(c, w_ada[0], b_ada[0])
    sh_a, sc_a, gt_a, sh_f, sc_f, gt_f = [m.reshape(B, 1, D) for m in jnp.split(mod, N_MOD, axis=-1)]
    y_rwkv, y_ret = _mixer(x, attn_norm_g[0], sc_a, sh_a, w_in[0].astype(BF16),
                           rwkv_mu[0], rwkv_w0[0], rwkv_w2[0], rwkv_a0[0], rwkv_a2[0], rwkv_g2[0],
                           rwkv_k_k[0], rwkv_k_a[0], rwkv_r_k[0], rwkv_ln_g[0], rwkv_ln_b[0],
                           cos2, sin2, dmw, qdec, kdec, gcw, ret_gn_g[0], mask)
    return _ffn(x, y_rwkv, y_ret, gt_a, sh_f, sc_f, gt_f, w_out[0].astype(BF16), ffn_norm_g[0],
                ffn_w_up[0].astype(BF16), ffn_conv_w[0], ffn_conv_b[0], ffn_w_down[0].astype(BF16),
                final_norm_g)
```

```python
import functools
import math

import jax
import jax.numpy as jnp
import numpy as np
from jax import lax
from jax.experimental import pallas as pl
from jax.experimental.pallas import tpu as pltpu

F32 = jnp.float32
BF16 = jnp.bfloat16

D_MODEL = 1024
CHUNK = 64
HEAD_DIM = 64
RWKV_WIDTH = 512
RET_WIDTH = 512
N_HEADS = 8
DECAY_LORA = 64
AAA_LORA = 64
GATE_LORA = 128
RWKV_COLS = 3 * RWKV_WIDTH + DECAY_LORA + AAA_LORA + GATE_LORA
RET_COLS = 4 * RET_WIDTH
D_FF = 2816
ROPE_BASE = 10000.0
NORM_EPS = 1e-6
RWKV_LN_EPS = 64e-5
RET_GN_EPS = 1e-6
W_DECAY_SCALE = math.exp(-0.5)
N_MOD = 6
HEADS_PER_GROUP = 4
GROUP = HEADS_PER_GROUP * HEAD_DIM

VMEM_LIMIT_BYTES = 56 * 1024 * 1024
MIXER_TOKENS = 512
FFN_ROWS = 512
FFN_COLS = 256
ADA_COLS = 512

_NT = (((1,), (1,)), ((), ()))


def _bdot(a, b):
    return jnp.dot(a.astype(BF16), b.astype(BF16), preferred_element_type=F32)


def _split_parts(x, parts=3):
    out, rem = [], x
    for i in range(parts):
        p = rem.astype(BF16)
        out.append(p)
        if i + 1 < parts:
            rem = rem - p.astype(F32)
    return out


def _head_sums(x, mask, parts=2):
    outs = []
    for g in range(x.shape[1] // GROUP):
        acc = None
        for p in _split_parts(x[:, g * GROUP:(g + 1) * GROUP], parts=parts):
            d = jnp.dot(p, mask, preferred_element_type=F32)
            acc = d if acc is None else acc + d
        outs.append(acc)
    return jnp.concatenate(outs, axis=1)


def _exact_dot_left(m_bf16, x):
    acc = None
    for p in _split_parts(x):
        d = jnp.dot(m_bf16, p, preferred_element_type=F32)
        acc = d if acc is None else acc + d
    return acc


def _sigmoid(x):
    return 1.0 / (1.0 + jnp.exp(-x))


def _rms(x):
    return x * lax.rsqrt(jnp.mean(x * x, axis=-1, keepdims=True) + NORM_EPS)


def _ada_kernel(c_ref, w_ref, b_ref, o_ref):
    cv = c_ref[...]
    s = cv * _sigmoid(cv)
    o_ref[...] = jnp.dot(s, w_ref[...], preferred_element_type=F32,
                         precision=lax.Precision.HIGHEST) + b_ref[...]


def _ada(c, w, b):
    B = c.shape[0]
    n = w.shape[1]
    return pl.pallas_call(
        _ada_kernel,
        grid=(n // ADA_COLS,),
        in_specs=[pl.BlockSpec((B, D_MODEL), lambda j: (0, 0)),
                  pl.BlockSpec((D_MODEL, ADA_COLS), lambda j: (0, j)),
                  pl.BlockSpec((1, ADA_COLS), lambda j: (0, j))],
        out_specs=pl.BlockSpec((B, ADA_COLS), lambda j: (0, j)),
        out_shape=jax.ShapeDtypeStruct((B, n), F32),
        compiler_params=pltpu.CompilerParams(vmem_limit_bytes=VMEM_LIMIT_BYTES),
        name="ada",
    )(c, w, b.reshape(1, n))


def _head_norm(y, mask, eps):
    mean = _head_sums(y, mask) * (1.0 / HEAD_DIM)
    d = y - mean
    var = _head_sums(d * d, mask, parts=1) * (1.0 / HEAD_DIM)
    return d * lax.rsqrt(var + eps)


def _shift_rows(x, first):
    rolled = pltpu.roll(x, 1, 0)
    head = jnp.where(lax.broadcasted_iota(jnp.int32, (8, 1), 0) == 0, first, rolled[0:8])
    return jnp.concatenate([head, rolled[8:]], axis=0)


def _bd(x, mask):
    xb = x.astype(BF16)
    half = GROUP // 2
    zeros = jnp.zeros((HEAD_DIM, half), BF16)
    blocks = []
    for h in range(HEADS_PER_GROUP):
        rows = slice(h * HEAD_DIM, (h + 1) * HEAD_DIM)
        if h < HEADS_PER_GROUP // 2:
            blocks.append(jnp.concatenate([xb[:, 0:half] * mask[rows, 0:half], zeros], axis=1))
        else:
            blocks.append(jnp.concatenate([zeros, xb[:, half:GROUP] * mask[rows, half:GROUP]], axis=1))
    return jnp.concatenate(blocks, axis=0)


def _block_transpose(x):
    r = jnp.concatenate([x] * HEADS_PER_GROUP, axis=0).T
    blk = lax.broadcasted_iota(jnp.int32, (1, GROUP), 1) // HEAD_DIM
    out = r[0:HEAD_DIM]
    for h in range(1, HEADS_PER_GROUP):
        out = jnp.where(blk == h, r[h * HEAD_DIM:(h + 1) * HEAD_DIM], out)
    return out


def _wdot(lhs, rhs_bd):
    return jnp.dot(lhs.astype(BF16), rhs_bd, preferred_element_type=F32)


def _wdot_nt(lhs, rhs_bd):
    return lax.dot_general(lhs.astype(BF16), rhs_bd, _NT, preferred_element_type=F32)


def _interleave(*gens):
    results = [None] * len(gens)
    live = list(range(len(gens)))
    while live:
        for i in list(live):
            try:
                next(gens[i])
            except StopIteration as stop:
                results[i] = stop.value
                live.remove(i)
    return results


def _rwkv_prep(z, zfirst, prm, mask):
    W = RWKV_WIDTH
    zprev = _shift_rows(z, zfirst)
    zm = z + prm["mu"] * (zprev - z)
    r = zm[:, 0:W]
    k = zm[:, W:2 * W]
    v = zm[:, 2 * W:3 * W]
    wd = zm[:, 3 * W:3 * W + DECAY_LORA]
    ad = zm[:, 3 * W + DECAY_LORA:3 * W + DECAY_LORA + AAA_LORA]
    gd = zm[:, 3 * W + DECAY_LORA + AAA_LORA:RWKV_COLS]

    lw = -W_DECAY_SCALE * _sigmoid(prm["w0"] + _bdot(jnp.tanh(wd), prm["w2"]))
    a = _sigmoid(prm["a0"] + _bdot(ad, prm["a2"]))
    g = _bdot(_sigmoid(gd), prm["g2"])
    kkr = k * prm["k_k"]
    yield
    kkn = kkr * lax.rsqrt(jnp.maximum(_head_sums(kkr * kkr, mask, parts=1), 1e-24))
    km = k * (1.0 + (a - 1.0) * prm["k_a"])
    yield
    bonus = _head_sums(r * km * prm["r_k"], mask) * v
    return dict(r=r, v=v, lw=lw, kkn=kkn, km=km, kb=kkn * a, bonus=bonus, g=g)


def _rwkv_chunk_operands(head, c):
    C = CHUNK
    rows = slice(c * C, (c + 1) * C)
    ri = lax.broadcasted_iota(jnp.int32, (C, C), 0)
    ci = lax.broadcasted_iota(jnp.int32, (C, C), 1)
    lw_c = head["lw"][rows]
    lp = _exact_dot_left((ri >= ci).astype(BF16), lw_c)
    em = jnp.exp(-lp)
    lpc = lp[C - 1:C, :]
    ee = jnp.exp(lpc - lp)
    kb, km = head["kb"][rows], head["km"][rows]
    fields = dict(rt=head["r"][rows] * jnp.exp(lp), at=-head["kkn"][rows] * jnp.exp(lp - lw_c),
                  bt=kb * em, kt=km * em, b_d=kb * ee, k_d=km * ee, v=head["v"][rows])
    return fields, lpc


def _rwkv_solve(get, hcur, mask, n_chunks):
    C = CHUNK
    n_groups = RWKV_WIDTH // GROUP
    rw = lax.broadcasted_iota(jnp.int32, (C, GROUP), 0)
    cw = lax.broadcasted_iota(jnp.int32, (C, GROUP), 1) % HEAD_DIM
    strict = rw > cw
    incl = rw >= cw
    eye = (rw == cw).astype(F32)

    def scores(u):
        lhs = jnp.concatenate([u["at"], u["rt"]], axis=0)
        sb = _wdot_nt(lhs, _bd(u["bt"], mask))
        sk = _wdot_nt(lhs, _bd(u["kt"], mask))
        u["p"] = jnp.where(strict, sb[0:C], 0.0)
        u["m_rb"] = jnp.where(incl, sb[C:2 * C], 0.0)
        u["l_ak"] = jnp.where(strict, sk[0:C], 0.0)
        u["m_rk"] = jnp.where(incl, sk[C:2 * C], 0.0)
        u["tinv"] = eye + u["p"]
        u["bt_b"] = _block_transpose(u["b_d"])
        u["bt_k"] = _block_transpose(u["k_d"])
        u["pcw"] = _block_transpose(u["pc"])

    def square(u):
        u["p"] = _wdot(u["p"], _bd(u["p"], mask))

    def double(u):
        both = _wdot(jnp.concatenate([u["p"], u["tinv"]], axis=0), _bd(u["p"], mask))
        u["p"] = both[0:C]
        u["tinv"] = u["tinv"] + both[C:2 * C]

    def double_last(u):
        u["tinv"] = u["tinv"] + _wdot(u["tinv"], _bd(u["p"], mask))

    def with_v(u):
        rv = _wdot(jnp.concatenate([u["l_ak"], u["m_rk"], u["bt_k"]], axis=0), _bd(u["v"], mask))
        u["x"], u["y0"], u["psi"] = rv[0:C], rv[C:2 * C], rv[2 * C:3 * C]

    def solve_w(u):
        u["w"] = _wdot(u["tinv"], _bd(u["at"], mask))
        u["u0"] = _wdot(u["tinv"], _bd(u["x"], mask))

    def fold(u):
        lhs = jnp.concatenate([u["bt_b"], u["m_rb"]], axis=0)
        rw_ = _wdot(lhs, _bd(u["w"], mask))
        ru = _wdot(lhs, _bd(u["u0"], mask))
        u["po"] = jnp.concatenate([rw_[0:C], u["rt"] + rw_[C:2 * C]], axis=0)
        u["psi"] = u["psi"] + ru[0:C]
        u["y0"] = u["y0"] + ru[C:2 * C]

    hcur = list(hcur)
    ys = [[None] * n_groups for _ in range(n_chunks)]

    def carry(u):
        c, gi = u["c"], u["g"]
        ph = _wdot(u["po"], _bd(hcur[gi], mask))
        ys[c][gi] = ph[C:2 * C] + u["y0"]
        hcur[gi] = u["pcw"] * hcur[gi] + ph[0:C] + u["psi"]

    stages = [scores, square, double, double, double, double, double_last, with_v, solve_w, fold, carry]

    units = {}
    for t in range(n_chunks + len(stages)):
        for c in range(n_chunks):
            st = t - c - 1
            if st == -1:
                fields, lpc = get(c)
                pc = jnp.broadcast_to(jnp.exp(lpc), (C, RWKV_WIDTH))
                units[c] = []
                for gi in range(n_groups):
                    ln = slice(gi * GROUP, (gi + 1) * GROUP)
                    u = {name: val[:, ln] for name, val in fields.items()}
                    u.update(c=c, g=gi, pc=pc[:, ln])
                    units[c].append(u)
            elif 0 <= st < len(stages):
                for u in units[c]:
                    stages[st](u)
        yield
    y = jnp.concatenate([jnp.concatenate(ys[c], axis=1) for c in range(n_chunks)], axis=0)
    return y, hcur


def _ret_mix(z_ref, cos_ref, sin_ref, dm_ref, qdec_ref, kdec_ref, gc_ref, scur, mask):
    C = CHUNK
    W = RET_WIDTH
    T = z_ref.shape[0]
    n_chunks = T // C
    n_groups = W // GROUP
    reps = W // cos_ref.shape[1]
    cosf = jnp.concatenate([cos_ref[...]] * reps, axis=1)
    sinf = jnp.concatenate([sin_ref[...]] * reps, axis=1)
    lane = lax.broadcasted_iota(jnp.int32, (1, W), 1)
    first_half = (lane % HEAD_DIM) < (HEAD_DIM // 2)

    def rot(t):
        swapped = jnp.where(first_half, pltpu.roll(t, W - HEAD_DIM // 2, 1), pltpu.roll(t, HEAD_DIM // 2, 1))
        return t * cosf + swapped * sinf

    q = rot(z_ref[:, 0:W]) * (HEAD_DIM ** -0.5)
    yield
    k = rot(z_ref[:, W:2 * W])
    v = z_ref[:, 2 * W:3 * W]
    yield

    units = []
    for c in range(n_chunks):
        rows = slice(c * C, (c + 1) * C)
        qd = q[rows] * qdec_ref[...]
        kd = k[rows] * kdec_ref[...]
        for gi in range(n_groups):
            ln = slice(gi * GROUP, (gi + 1) * GROUP)
            units.append(dict(c=c, g=gi, q=q[rows, ln], k=k[rows, ln], v=v[rows, ln], qd=qd[:, ln], kd=kd[:, ln]))
    for u in units:
        u["sc"] = _wdot_nt(u["q"], _bd(u["k"], mask)) * dm_ref[:, u["g"] * GROUP:(u["g"] + 1) * GROUP]
        u["kdt"] = _block_transpose(u["kd"])
    yield
    for u in units:
        both = _wdot(jnp.concatenate([u["sc"], u["kdt"]], axis=0), _bd(u["v"], mask))
        u["y"], u["kv"] = both[0:C], both[C:2 * C]
    yield

    scur = list(scur)
    ys = [[None] * n_groups for _ in range(n_chunks)]
    for u in units:
        c, gi = u["c"], u["g"]
        ys[c][gi] = u["y"] + _wdot(u["qd"], _bd(scur[gi], mask))
        scur[gi] = scur[gi] * gc_ref[:, gi * GROUP:(gi + 1) * GROUP] + u["kv"]
    y = jnp.concatenate([jnp.concatenate(ys[c], axis=1) for c in range(n_chunks)], axis=0)
    return y, scur


def _mixer_kernel(x_ref, g_ref, sc_ref, sh_ref, win_ref,
                  mu_ref, w0_ref, w2_ref, a0_ref, a2_ref, g2_ref, kk_ref, ka_ref, rk_ref, lng_ref, lnb_ref,
                  cos_ref, sin_ref, dm_ref, qdec_ref, kdec_ref, gc_ref, gng_ref, mask_ref,
                  yr_ref, yt_ref, zr_ref, zt_ref, zlast_ref, rstate_ref, tstate_ref):
    T = x_ref.shape[1]
    n_groups = RWKV_WIDTH // GROUP

    @pl.when(pl.program_id(1) == 0)
    def _():
        zlast_ref[...] = jnp.zeros_like(zlast_ref)
        rstate_ref[...] = jnp.zeros_like(rstate_ref)
        tstate_ref[...] = jnp.zeros_like(tstate_ref)

    h = (_rms(x_ref[0]) * g_ref[...] * (1.0 + sc_ref[0]) + sh_ref[0]).astype(BF16)
    step = 256
    for j in range(0, RWKV_COLS, step):
        zr_ref[:, j:j + step] = jnp.dot(h, win_ref[:, j:j + step], preferred_element_type=F32)

    def ret_proj():
        for j in range(0, RET_COLS, step):
            zt_ref[:, j:j + step] = jnp.dot(h, win_ref[:, RWKV_COLS + j:RWKV_COLS + j + step],
                                            preferred_element_type=F32)
            yield

    mask = mask_ref[...]
    prm = dict(mu=mu_ref[...], w0=w0_ref[...], w2=w2_ref[...], a0=a0_ref[...], a2=a2_ref[...],
               g2=g2_ref[...], k_k=kk_ref[...], k_a=ka_ref[...], r_k=rk_ref[...])
    head, _ = _interleave(_rwkv_prep(zr_ref[...], zlast_ref[0:1, :], prm, mask), ret_proj())
    zlast_ref[0:1, :] = zr_ref[T - 1:T, :]

    (y, hcur), (yt, scur) = _interleave(
        _rwkv_solve(functools.partial(_rwkv_chunk_operands, head), [rstate_ref[gi] for gi in range(n_groups)],
                    mask, T // CHUNK),
        _ret_mix(zt_ref, cos_ref, sin_ref, dm_ref, qdec_ref, kdec_ref, gc_ref,
                 [tstate_ref[gi] for gi in range(n_groups)], mask))
    for gi in range(n_groups):
        rstate_ref[gi] = hcur[gi]
        tstate_ref[gi] = scur[gi]
    y = _head_norm(y, mask, RWKV_LN_EPS) * lng_ref[...] + lnb_ref[...]
    yr_ref[0] = ((y + head["bonus"]) * head["g"]).astype(BF16)
    gate = zt_ref[:, 3 * RET_WIDTH:4 * RET_WIDTH]
    yt_ref[0] = (gate * _sigmoid(gate) * (_head_norm(yt, mask, RET_GN_EPS) * gng_ref[...])).astype(BF16)


def _mixer(x, g, sc, sh, w_in, mu, w0, w2, a0, a2, g2, k_k, k_a, r_k, ln_g, ln_b,
           cos2, sin2, dmw, qdec, kdec, gcw, gn_g, mask):
    B, S, D = x.shape
    T = MIXER_TOKENS
    W = RWKV_WIDTH
    row = lambda b, t: (b, t, 0)
    per_b = lambda b, t: (b, 0, 0)
    vec = lambda a: a.reshape(1, -1)
    full = lambda a: pl.BlockSpec(a.shape, lambda b, t: (0, 0))
    tab = lambda a: pl.BlockSpec((T, a.shape[1]), lambda b, t: (t, 0))
    params = [vec(mu), vec(w0), w2.astype(BF16), vec(a0), a2.astype(BF16), g2.astype(BF16),
              vec(k_k), vec(k_a), vec(r_k), vec(ln_g), vec(ln_b)]
    tables = [dmw, qdec, kdec, gcw, vec(gn_g), mask]
    return pl.pallas_call(
        _mixer_kernel,
        grid=(B, S // T),
        in_specs=[pl.BlockSpec((1, T, D), row), full(vec(g)), pl.BlockSpec((1, 1, D), per_b),
                  pl.BlockSpec((1, 1, D), per_b), full(w_in)]
                 + [full(a) for a in params] + [tab(cos2), tab(sin2)] + [full(a) for a in tables],
        out_specs=[pl.BlockSpec((1, T, W), row), pl.BlockSpec((1, T, RET_WIDTH), row)],
        out_shape=[jax.ShapeDtypeStruct((B, S, W), BF16), jax.ShapeDtypeStruct((B, S, RET_WIDTH), BF16)],
        scratch_shapes=[pltpu.VMEM((T, RWKV_COLS), F32),
                        pltpu.VMEM((T, RET_COLS), F32),
                        pltpu.VMEM((8, RWKV_COLS), F32),
                        pltpu.VMEM((W // GROUP, HEAD_DIM, GROUP), F32),
                        pltpu.VMEM((RET_WIDTH // GROUP, HEAD_DIM, GROUP), F32)],
        compiler_params=pltpu.CompilerParams(
            dimension_semantics=("parallel", "arbitrary"), vmem_limit_bytes=VMEM_LIMIT_BYTES),
        name="mixer",
    )(x, vec(g), sc, sh, w_in, *params, cos2, sin2, *tables)


def _ffn_kernel(x_ref, yr_ref, yt_ref, gta_ref, shf_ref, scf_ref, gtf_ref, wout_ref, gnf_ref,
                wup_ref, cw_ref, cb_ref, wdn_ref, gfin_ref, o_ref, carry_ref, ubuf_ref, abuf_ref):
    tm = x_ref.shape[1]

    @pl.when(pl.program_id(1) == 0)
    def _():
        carry_ref[...] = jnp.zeros_like(carry_ref)

    ymix = (jnp.dot(yr_ref[0], wout_ref[0:RWKV_WIDTH, :], preferred_element_type=F32)
            + jnp.dot(yt_ref[0], wout_ref[RWKV_WIDTH:, :], preferred_element_type=F32))
    x1 = x_ref[0] + gta_ref[0] * ymix
    h = (_rms(x1) * gnf_ref[...] * (1.0 + scf_ref[0]) + shf_ref[0]).astype(BF16)
    row8 = lax.broadcasted_iota(jnp.int32, (8, 1), 0)

    def conv(u, col):
        tf = u.shape[1]
        prev = carry_ref[:, col:col + tf]
        r1, r2 = pltpu.roll(u, 1, 0), pltpu.roll(u, 2, 0)
        h1 = jnp.where(row8 == 0, prev[7:8, :], r1[0:8])
        h2 = jnp.where(row8 == 0, prev[6:7, :], jnp.where(row8 == 1, prev[7:8, :], r2[0:8]))
        u1 = jnp.concatenate([h1, r1[8:]], axis=0)
        u2 = jnp.concatenate([h2, r2[8:]], axis=0)
        carry_ref[:, col:col + tf] = u[tm - 8:tm, :]
        cw = cw_ref[:, col:col + tf]
        return cb_ref[:, col:col + tf] + u2 * cw[0:1, :] + u1 * cw[1:2, :] + u * cw[2:3, :]

    tf = FFN_COLS
    tiles = list(range(0, D_FF, tf))

    def up(n):
        j, slot = tiles[n], n % 2
        ubuf_ref[slot, :, 0:tf] = jnp.dot(h, wup_ref[:, j:j + tf], preferred_element_type=F32)
        ubuf_ref[slot, :, tf:2 * tf] = jnp.dot(h, wup_ref[:, D_FF + j:D_FF + j + tf], preferred_element_type=F32)

    def glu(n):
        j, slot = tiles[n], n % 2
        val = conv(ubuf_ref[slot, :, 0:tf], j)
        gate = conv(ubuf_ref[slot, :, tf:2 * tf], D_FF + j)
        abuf_ref[:, j:j + tf] = (gate * _sigmoid(gate) * val).astype(BF16)

    up(0)
    for n in range(1, len(tiles)):
        up(n)
        glu(n - 1)
    glu(len(tiles) - 1)
    y = jnp.dot(abuf_ref[...], wdn_ref[...], preferred_element_type=F32)
    x2 = x1 + gtf_ref[0] * y
    o_ref[0] = _rms(x2) * gfin_ref[...]


def _ffn(x, y_rwkv, y_ret, gt_a, sh_f, sc_f, gt_f, w_out, g_ffn, w_up, conv_w, conv_b, w_down, g_fin):
    B, S, D = x.shape
    tm = FFN_ROWS
    row = lambda b, t: (b, t, 0)
    per_b = lambda b, t: (b, 0, 0)
    const = lambda b, t: (0, 0)
    once = lambda a: pl.BlockSpec(a.shape, const, pipeline_mode=pl.Buffered(1))
    vecs = [g_ffn.reshape(1, D)]
    return pl.pallas_call(
        _ffn_kernel,
        grid=(B, S // tm),
        in_specs=[pl.BlockSpec((1, tm, D), row),
                  pl.BlockSpec((1, tm, RWKV_WIDTH), row),
                  pl.BlockSpec((1, tm, RET_WIDTH), row),
                  pl.BlockSpec((1, 1, D), per_b), pl.BlockSpec((1, 1, D), per_b),
                  pl.BlockSpec((1, 1, D), per_b), pl.BlockSpec((1, 1, D), per_b),
                  once(w_out), once(vecs[0]), once(w_up), once(conv_w), pl.BlockSpec((1, 2 * D_FF), const),
                  once(w_down), pl.BlockSpec((1, D), const)],
        out_specs=pl.BlockSpec((1, tm, D), row),
        out_shape=jax.ShapeDtypeStruct((B, S, D), F32),
        scratch_shapes=[pltpu.VMEM((8, 2 * D_FF), F32),
                        pltpu.VMEM((2, tm, 2 * FFN_COLS), F32),
                        pltpu.VMEM((tm, D_FF), BF16)],
        compiler_params=pltpu.CompilerParams(
            dimension_semantics=("parallel", "arbitrary"), vmem_limit_bytes=VMEM_LIMIT_BYTES),
        name="ffn",
    )(x, y_rwkv, y_ret, gt_a, sh_f, sc_f, gt_f, w_out, vecs[0], w_up, conv_w,
      conv_b.reshape(1, 2 * D_FF), w_down, g_fin.reshape(1, D))


def _tables(S):
    pos = np.arange(S, dtype=np.float64)
    inv_freq = ROPE_BASE ** (-np.arange(0, HEAD_DIM, 2, dtype=np.float64) / HEAD_DIM)
    ang = pos[:, None] * inv_freq[None, :]
    cos, sin = np.cos(ang), np.sin(ang)
    cos2 = np.concatenate([cos, cos, cos, cos], axis=1)
    sin2 = np.concatenate([-sin, sin, -sin, sin], axis=1)
    log_gamma = np.log1p(-(2.0 ** (-5.0 - np.arange(N_HEADS, dtype=np.float64))))
    idx = np.arange(CHUNK, dtype=np.float64)
    dmat = np.exp(log_gamma[:, None, None] * np.abs(idx[:, None] - idx[None, :]))
    q_dec = np.exp(log_gamma[:, None] * (idx + 1.0))
    k_dec = np.exp(log_gamma[:, None] * (CHUNK - 1.0 - idx))
    widen = lambda t: np.repeat(t.T, HEAD_DIM, axis=1)
    dmw = dmat.transpose(1, 0, 2).reshape(CHUNK, N_HEADS * CHUNK)
    gcw = np.repeat(np.exp(log_gamma * CHUNK), HEAD_DIM)[None, :]
    head = np.arange(GROUP) // HEAD_DIM
    mask = head[:, None] == head[None, :]
    f32 = lambda t: jnp.asarray(t, F32)
    return f32(cos2), f32(sin2), f32(dmw), f32(widen(q_dec)), f32(widen(k_dec)), f32(gcw), jnp.asarray(mask, BF16)


def kernel(x, c, w_ada, b_ada, attn_norm_g, w_in, rwkv_mu, rwkv_w0, rwkv_w2, rwkv_a0, rwkv_a2, rwkv_g2,
           rwkv_k_k, rwkv_k_a, rwkv_r_k, rwkv_ln_g, rwkv_ln_b, ret_gn_g, w_out, ffn_norm_g, ffn_w_up,
           ffn_conv_w, ffn_conv_b, ffn_w_down, final_norm_g):
    B, S, D = x.shape
    assert D == D_MODEL and S % MIXER_TOKENS == 0 and S % FFN_ROWS == 0 and w_ada.shape[0] == 1
    cos2, sin2, dmw, qdec, kdec, gcw, mask = _tables(S)
    mod = _ada(c, w_ada[0], b_ada[0])
    sh_a, sc_a, gt_a, sh_f, sc_f, gt_f = [m.reshape(B, 1, D) for m in jnp.split(mod, N_MOD, axis=-1)]
    y_rwkv, y_ret = _mixer(x, attn_norm_g[0], sc_a, sh_a, w_in[0].astype(BF16),
                           rwkv_mu[0], rwkv_w0[0], rwkv_w2[0], rwkv_a0[0], rwkv_a2[0], rwkv_g2[0],
                           rwkv_k_k[0], rwkv_k_a[0], rwkv_r_k[0], rwkv_ln_g[0], rwkv_ln_b[0],
                           cos2, sin2, dmw, qdec, kdec, gcw, ret_gn_g[0], mask)
    return _ffn(x, y_rwkv, y_ret, gt_a, sh_f, sc_f, gt_f, w_out[0].astype(BF16), ffn_norm_g[0],
                ffn_w_up[0].astype(BF16), ffn_conv_w[0], ffn_conv_b[0], ffn_w_down[0].astype(BF16),
                final_norm_g)
```

```python
import functools
import math

import jax
import jax.numpy as jnp
import numpy as np
from jax import lax
from jax.experimental import pallas as pl
from jax.experimental.pallas import tpu as pltpu

F32 = jnp.float32
BF16 = jnp.bfloat16

D_MODEL = 1024
CHUNK = 64
HEAD_DIM = 64
RWKV_WIDTH = 512
RET_WIDTH = 512
N_HEADS = 8
DECAY_LORA = 64
AAA_LORA = 64
GATE_LORA = 128
RWKV_COLS = 3 * RWKV_WIDTH + DECAY_LORA + AAA_LORA + GATE_LORA
RET_COLS = 4 * RET_WIDTH
D_FF = 2816
ROPE_BASE = 10000.0
NORM_EPS = 1e-6
RWKV_LN_EPS = 64e-5
RET_GN_EPS = 1e-6
W_DECAY_SCALE = math.exp(-0.5)
N_MOD = 6
MXU_DIM = 256
SUBLANES = 8
GROUP = MXU_DIM
HEADS_PER_GROUP = GROUP // HEAD_DIM
KK_NORM_FLOOR = 1e-12

VMEM_LIMIT_BYTES = 56 * 1024 * 1024
MIXER_TOKENS = 512
FFN_ROWS = 512
FFN_COLS = MXU_DIM
INPROJ_COLS = MXU_DIM
ADA_COLS = 1536

_NT = (((1,), (1,)), ((), ()))


def _bdot(a, b):
    return jnp.dot(a.astype(BF16), b.astype(BF16), preferred_element_type=F32)


def _split_parts(x, parts=3):
    out, rem = [], x
    for i in range(parts):
        p = rem.astype(BF16)
        out.append(p)
        if i + 1 < parts:
            rem = rem - p.astype(F32)
    return out


def _head_sums(x, mask, parts=2):
    outs = []
    for g in range(x.shape[1] // GROUP):
        acc = None
        for p in _split_parts(x[:, g * GROUP:(g + 1) * GROUP], parts=parts):
            d = jnp.dot(p, mask, preferred_element_type=F32)
            acc = d if acc is None else acc + d
        outs.append(acc)
    return jnp.concatenate(outs, axis=1)


def _exact_dot_left(m_bf16, x):
    acc = None
    for p in _split_parts(x):
        d = jnp.dot(m_bf16, p, preferred_element_type=F32)
        acc = d if acc is None else acc + d
    return acc


def _sigmoid(x):
    return 1.0 / (1.0 + jnp.exp(-x))


def _rms(x):
    return x * lax.rsqrt(jnp.mean(x * x, axis=-1, keepdims=True) + NORM_EPS)


def _ada_kernel(c_ref, w_ref, b_ref, o_ref):
    cv = c_ref[...]
    s = cv * _sigmoid(cv)
    o_ref[...] = jnp.dot(s, w_ref[...], preferred_element_type=F32,
                         precision=lax.Precision.HIGHEST) + b_ref[...]


def _ada(c, w, b):
    B = c.shape[0]
    n = w.shape[1]
    return pl.pallas_call(
        _ada_kernel,
        grid=(n // ADA_COLS,),
        in_specs=[pl.BlockSpec((B, D_MODEL), lambda j: (0, 0)),
                  pl.BlockSpec((D_MODEL, ADA_COLS), lambda j: (0, j)),
                  pl.BlockSpec((1, ADA_COLS), lambda j: (0, j))],
        out_specs=pl.BlockSpec((B, ADA_COLS), lambda j: (0, j)),
        out_shape=jax.ShapeDtypeStruct((B, n), F32),
        compiler_params=pltpu.CompilerParams(vmem_limit_bytes=VMEM_LIMIT_BYTES),
        name="ada",
    )(c, w, b.reshape(1, n))


def _head_norm(y, mask, eps):
    mean = _head_sums(y, mask) * (1.0 / HEAD_DIM)
    d = y - mean
    var = _head_sums(d * d, mask, parts=1) * (1.0 / HEAD_DIM)
    return d * lax.rsqrt(var + eps)


def _shift_rows(x, first):
    rolled = pltpu.roll(x, 1, 0)
    head = jnp.where(lax.broadcasted_iota(jnp.int32, (SUBLANES, 1), 0) == 0, first, rolled[0:SUBLANES])
    return jnp.concatenate([head, rolled[SUBLANES:]], axis=0)


def _bd(x, mask):
    xb = x.astype(BF16)
    half = GROUP // 2
    zeros = jnp.zeros((HEAD_DIM, half), BF16)
    blocks = []
    for h in range(HEADS_PER_GROUP):
        rows = slice(h * HEAD_DIM, (h + 1) * HEAD_DIM)
        if h < HEADS_PER_GROUP // 2:
            blocks.append(jnp.concatenate([xb[:, 0:half] * mask[rows, 0:half], zeros], axis=1))
        else:
            blocks.append(jnp.concatenate([zeros, xb[:, half:GROUP] * mask[rows, half:GROUP]], axis=1))
    return jnp.concatenate(blocks, axis=0)


def _block_transpose(x):
    r = jnp.concatenate([x] * HEADS_PER_GROUP, axis=0).T
    blk = lax.broadcasted_iota(jnp.int32, (1, GROUP), 1) // HEAD_DIM
    out = r[0:HEAD_DIM]
    for h in range(1, HEADS_PER_GROUP):
        out = jnp.where(blk == h, r[h * HEAD_DIM:(h + 1) * HEAD_DIM], out)
    return out


def _wdot(lhs, rhs_bd):
    return jnp.dot(lhs.astype(BF16), rhs_bd, preferred_element_type=F32)


def _wdot_nt(lhs, rhs_bd):
    return lax.dot_general(lhs.astype(BF16), rhs_bd, _NT, preferred_element_type=F32)


def _interleave(*gens):
    results = [None] * len(gens)
    live = list(range(len(gens)))
    while live:
        for i in list(live):
            try:
                next(gens[i])
            except StopIteration as stop:
                results[i] = stop.value
                live.remove(i)
    return results


def _rwkv_prep(z, zfirst, prm, mask):
    W = RWKV_WIDTH
    zprev = _shift_rows(z, zfirst)
    zm = z + prm["mu"] * (zprev - z)
    r = zm[:, 0:W]
    k = zm[:, W:2 * W]
    v = zm[:, 2 * W:3 * W]
    wd = zm[:, 3 * W:3 * W + DECAY_LORA]
    ad = zm[:, 3 * W + DECAY_LORA:3 * W + DECAY_LORA + AAA_LORA]
    gd = zm[:, 3 * W + DECAY_LORA + AAA_LORA:RWKV_COLS]

    lw = -W_DECAY_SCALE * _sigmoid(prm["w0"] + _bdot(jnp.tanh(wd), prm["w2"]))
    a = _sigmoid(prm["a0"] + _bdot(ad, prm["a2"]))
    g = _bdot(_sigmoid(gd), prm["g2"])
    kkr = k * prm["k_k"]
    yield
    kkn = kkr * lax.rsqrt(jnp.maximum(_head_sums(kkr * kkr, mask, parts=1), KK_NORM_FLOOR ** 2))
    km = k * (1.0 + (a - 1.0) * prm["k_a"])
    yield
    bonus = _head_sums(r * km * prm["r_k"], mask) * v
    return dict(r=r, v=v, lw=lw, kkn=kkn, km=km, kb=kkn * a, bonus=bonus, g=g)


def _rwkv_chunk_operands(head, c):
    C = CHUNK
    rows = slice(c * C, (c + 1) * C)
    ri = lax.broadcasted_iota(jnp.int32, (C, C), 0)
    ci = lax.broadcasted_iota(jnp.int32, (C, C), 1)
    lw_c = head["lw"][rows]
    lp = _exact_dot_left((ri >= ci).astype(BF16), lw_c)
    em = jnp.exp(-lp)
    lpc = lp[C - 1:C, :]
    ee = jnp.exp(lpc - lp)
    kb, km = head["kb"][rows], head["km"][rows]
    fields = dict(rt=head["r"][rows] * jnp.exp(lp), at=-head["kkn"][rows] * jnp.exp(lp - lw_c),
                  bt=kb * em, kt=km * em, b_d=kb * ee, k_d=km * ee, v=head["v"][rows])
    return fields, lpc


def _rwkv_solve(get, hcur, mask, n_chunks):
    C = CHUNK
    n_groups = RWKV_WIDTH // GROUP
    rw = lax.broadcasted_iota(jnp.int32, (C, GROUP), 0)
    cw = lax.broadcasted_iota(jnp.int32, (C, GROUP), 1) % HEAD_DIM
    strict = rw > cw
    incl = rw >= cw
    eye = (rw == cw).astype(F32)

    def scores(u):
        lhs = jnp.concatenate([u["at"], u["rt"]], axis=0)
        sb = _wdot_nt(lhs, _bd(u["bt"], mask))
        sk = _wdot_nt(lhs, _bd(u["kt"], mask))
        u["p"] = jnp.where(strict, sb[0:C], 0.0)
        u["m_rb"] = jnp.where(incl, sb[C:2 * C], 0.0)
        u["l_ak"] = jnp.where(strict, sk[0:C], 0.0)
        u["m_rk"] = jnp.where(incl, sk[C:2 * C], 0.0)
        u["tinv"] = eye + u["p"]
        u["bt_b"] = _block_transpose(u["b_d"])
        u["bt_k"] = _block_transpose(u["k_d"])
        u["pcw"] = _block_transpose(u["pc"])

    def square(u):
        u["p"] = _wdot(u["p"], _bd(u["p"], mask))

    def double(u):
        both = _wdot(jnp.concatenate([u["p"], u["tinv"]], axis=0), _bd(u["p"], mask))
        u["p"] = both[0:C]
        u["tinv"] = u["tinv"] + both[C:2 * C]

    def double_last(u):
        u["tinv"] = u["tinv"] + _wdot(u["tinv"], _bd(u["p"], mask))

    def with_v(u):
        rv = _wdot(jnp.concatenate([u["l_ak"], u["m_rk"], u["bt_k"]], axis=0), _bd(u["v"], mask))
        u["x"], u["y0"], u["psi"] = rv[0:C], rv[C:2 * C], rv[2 * C:3 * C]

    def times_t(u):
        u["gt"] = _wdot(jnp.concatenate([u["bt_b"], u["m_rb"]], axis=0), _bd(u["tinv"], mask))

    def fold(u):
        rw_ = _wdot(u["gt"], _bd(u["at"], mask))
        ru = _wdot(u["gt"], _bd(u["x"], mask))
        u["po"] = jnp.concatenate([rw_[0:C], u["rt"] + rw_[C:2 * C]], axis=0)
        u["psi"] = u["psi"] + ru[0:C]
        u["y0"] = u["y0"] + ru[C:2 * C]

    hcur = list(hcur)
    ys = [[None] * n_groups for _ in range(n_chunks)]

    def carry(u):
        c, gi = u["c"], u["g"]
        ph = _wdot(u["po"], _bd(hcur[gi], mask))
        ys[c][gi] = ph[C:2 * C] + u["y0"]
        hcur[gi] = u["pcw"] * hcur[gi] + ph[0:C] + u["psi"]

    stages = [scores, square, double, double, double, double, double_last, with_v, times_t, fold, carry]

    units = {}
    for t in range(n_chunks + len(stages)):
        for c in range(n_chunks):
            st = t - c - 1
            if st == -1:
                fields, lpc = get(c)
                pc = jnp.broadcast_to(jnp.exp(lpc), (C, RWKV_WIDTH))
                units[c] = []
                for gi in range(n_groups):
                    ln = slice(gi * GROUP, (gi + 1) * GROUP)
                    u = {name: val[:, ln] for name, val in fields.items()}
                    u.update(c=c, g=gi, pc=pc[:, ln])
                    units[c].append(u)
            elif 0 <= st < len(stages):
                for u in units[c]:
                    stages[st](u)
        yield
    y = jnp.concatenate([jnp.concatenate(ys[c], axis=1) for c in range(n_chunks)], axis=0)
    return y, hcur


def _ret_mix(z_ref, cos_ref, sin_ref, dm_ref, qdec_ref, kdec_ref, gc_ref, scur, mask):
    C = CHUNK
    W = RET_WIDTH
    T = z_ref.shape[0]
    n_chunks = T // C
    n_groups = W // GROUP
    reps = W // cos_ref.shape[1]
    cosf = jnp.concatenate([cos_ref[...]] * reps, axis=1)
    sinf = jnp.concatenate([sin_ref[...]] * reps, axis=1)
    lane = lax.broadcasted_iota(jnp.int32, (1, W), 1)
    first_half = (lane % HEAD_DIM) < (HEAD_DIM // 2)

    def rot(t):
        swapped = jnp.where(first_half, pltpu.roll(t, W - HEAD_DIM // 2, 1), pltpu.roll(t, HEAD_DIM // 2, 1))
        return t * cosf + swapped * sinf

    q = rot(z_ref[:, 0:W]) * (HEAD_DIM ** -0.5)
    yield
    k = rot(z_ref[:, W:2 * W])
    v = z_ref[:, 2 * W:3 * W]
    yield

    units = []
    for c in range(n_chunks):
        rows = slice(c * C, (c + 1) * C)
        qd = q[rows] * qdec_ref[...]
        kd = k[rows] * kdec_ref[...]
        for gi in range(n_groups):
            ln = slice(gi * GROUP, (gi + 1) * GROUP)
            units.append(dict(c=c, g=gi, q=q[rows, ln], k=k[rows, ln], v=v[rows, ln], qd=qd[:, ln], kd=kd[:, ln]))
    for u in units:
        u["sc"] = _wdot_nt(u["q"], _bd(u["k"], mask)) * dm_ref[:, u["g"] * GROUP:(u["g"] + 1) * GROUP]
        u["kdt"] = _block_transpose(u["kd"])
    yield
    for u in units:
        both = _wdot(jnp.concatenate([u["sc"], u["kdt"]], axis=0), _bd(u["v"], mask))
        u["y"], u["kv"] = both[0:C], both[C:2 * C]
    yield

    scur = list(scur)
    ys = [[None] * n_groups for _ in range(n_chunks)]
    for u in units:
        c, gi = u["c"], u["g"]
        ys[c][gi] = u["y"] + _wdot(u["qd"], _bd(scur[gi], mask))
        scur[gi] = scur[gi] * gc_ref[:, gi * GROUP:(gi + 1) * GROUP] + u["kv"]
    y = jnp.concatenate([jnp.concatenate(ys[c], axis=1) for c in range(n_chunks)], axis=0)
    return y, scur


def _mixer_kernel(x_ref, g_ref, sc_ref, sh_ref, win_ref,
                  mu_ref, w0_ref, w2_ref, a0_ref, a2_ref, g2_ref, kk_ref, ka_ref, rk_ref, lng_ref, lnb_ref,
                  cos_ref, sin_ref, dm_ref, qdec_ref, kdec_ref, gc_ref, gng_ref, mask_ref,
                  yr_ref, yt_ref, zr_ref, zt_ref, zlast_ref, rstate_ref, tstate_ref):
    T = x_ref.shape[1]
    n_groups = RWKV_WIDTH // GROUP

    @pl.when(pl.program_id(1) == 0)
    def _():
        zlast_ref[...] = jnp.zeros_like(zlast_ref)
        rstate_ref[...] = jnp.zeros_like(rstate_ref)
        tstate_ref[...] = jnp.zeros_like(tstate_ref)

    h = (_rms(x_ref[0]) * g_ref[...] * (1.0 + sc_ref[0]) + sh_ref[0]).astype(BF16)
    step = INPROJ_COLS
    for j in range(0, RWKV_COLS, step):
        zr_ref[:, j:j + step] = jnp.dot(h, win_ref[:, j:j + step], preferred_element_type=F32)

    def ret_proj():
        for j in range(0, RET_COLS, step):
            zt_ref[:, j:j + step] = jnp.dot(h, win_ref[:, RWKV_COLS + j:RWKV_COLS + j + step],
                                            preferred_element_type=F32)
            yield

    mask = mask_ref[...]
    prm = dict(mu=mu_ref[...], w0=w0_ref[...], w2=w2_ref[...], a0=a0_ref[...], a2=a2_ref[...],
               g2=g2_ref[...], k_k=kk_ref[...], k_a=ka_ref[...], r_k=rk_ref[...])
    head, _ = _interleave(_rwkv_prep(zr_ref[...], zlast_ref[0:1, :], prm, mask), ret_proj())
    zlast_ref[0:1, :] = zr_ref[T - 1:T, :]

    (y, hcur), (yt, scur) = _interleave(
        _rwkv_solve(functools.partial(_rwkv_chunk_operands, head), [rstate_ref[gi] for gi in range(n_groups)],
                    mask, T // CHUNK),
        _ret_mix(zt_ref, cos_ref, sin_ref, dm_ref, qdec_ref, kdec_ref, gc_ref,
                 [tstate_ref[gi] for gi in range(n_groups)], mask))
    for gi in range(n_groups):
        rstate_ref[gi] = hcur[gi]
        tstate_ref[gi] = scur[gi]
    y = _head_norm(y, mask, RWKV_LN_EPS) * lng_ref[...] + lnb_ref[...]
    yr_ref[0] = ((y + head["bonus"]) * head["g"]).astype(BF16)
    gate = zt_ref[:, 3 * RET_WIDTH:4 * RET_WIDTH]
    yt_ref[0] = (gate * _sigmoid(gate) * (_head_norm(yt, mask, RET_GN_EPS) * gng_ref[...])).astype(BF16)


def _mixer(x, g, sc, sh, w_in, mu, w0, w2, a0, a2, g2, k_k, k_a, r_k, ln_g, ln_b,
           cos2, sin2, dmw, qdec, kdec, gcw, gn_g, mask):
    B, S, D = x.shape
    T = MIXER_TOKENS
    W = RWKV_WIDTH
    row = lambda b, t: (b, t, 0)
    per_b = lambda b, t: (b, 0, 0)
    vec = lambda a: a.reshape(1, -1)
    full = lambda a: pl.BlockSpec(a.shape, lambda b, t: (0, 0))
    tab = lambda a: pl.BlockSpec((T, a.shape[1]), lambda b, t: (t, 0))
    params = [vec(mu), vec(w0), w2.astype(BF16), vec(a0), a2.astype(BF16), g2.astype(BF16),
              vec(k_k), vec(k_a), vec(r_k), vec(ln_g), vec(ln_b)]
    tables = [dmw, qdec, kdec, gcw, vec(gn_g), mask]
    return pl.pallas_call(
        _mixer_kernel,
        grid=(B, S // T),
        in_specs=[pl.BlockSpec((1, T, D), row), full(vec(g)), pl.BlockSpec((1, 1, D), per_b),
                  pl.BlockSpec((1, 1, D), per_b), full(w_in)]
                 + [full(a) for a in params] + [tab(cos2), tab(sin2)] + [full(a) for a in tables],
        out_specs=[pl.BlockSpec((1, T, W), row), pl.BlockSpec((1, T, RET_WIDTH), row)],
        out_shape=[jax.ShapeDtypeStruct((B, S, W), BF16), jax.ShapeDtypeStruct((B, S, RET_WIDTH), BF16)],
        scratch_shapes=[pltpu.VMEM((T, RWKV_COLS), F32),
                        pltpu.VMEM((T, RET_COLS), F32),
                        pltpu.VMEM((SUBLANES, RWKV_COLS), F32),
                        pltpu.VMEM((W // GROUP, HEAD_DIM, GROUP), F32),
                        pltpu.VMEM((RET_WIDTH // GROUP, HEAD_DIM, GROUP), F32)],
        compiler_params=pltpu.CompilerParams(
            dimension_semantics=("parallel", "arbitrary"), vmem_limit_bytes=VMEM_LIMIT_BYTES),
        name="mixer",
    )(x, vec(g), sc, sh, w_in, *params, cos2, sin2, *tables)


def _ffn_kernel(x_ref, yr_ref, yt_ref, gta_ref, shf_ref, scf_ref, gtf_ref, wout_ref, gnf_ref,
                wup_ref, cw_ref, cb_ref, wdn_ref, gfin_ref, o_ref, carry_ref, ubuf_ref, abuf_ref):
    tm = x_ref.shape[1]

    @pl.when(pl.program_id(1) == 0)
    def _():
        carry_ref[...] = jnp.zeros_like(carry_ref)

    ymix = (jnp.dot(yr_ref[0], wout_ref[0:RWKV_WIDTH, :], preferred_element_type=F32)
            + jnp.dot(yt_ref[0], wout_ref[RWKV_WIDTH:, :], preferred_element_type=F32))
    x1 = x_ref[0] + gta_ref[0] * ymix
    h = (_rms(x1) * gnf_ref[...] * (1.0 + scf_ref[0]) + shf_ref[0]).astype(BF16)
    row8 = lax.broadcasted_iota(jnp.int32, (SUBLANES, 1), 0)

    def conv(u, col):
        tf = u.shape[1]
        prev = carry_ref[:, col:col + tf]
        last, before = prev[SUBLANES - 1:SUBLANES, :], prev[SUBLANES - 2:SUBLANES - 1, :]
        r1, r2 = pltpu.roll(u, 1, 0), pltpu.roll(u, 2, 0)
        h1 = jnp.where(row8 == 0, last, r1[0:SUBLANES])
        h2 = jnp.where(row8 == 0, before, jnp.where(row8 == 1, last, r2[0:SUBLANES]))
        u1 = jnp.concatenate([h1, r1[SUBLANES:]], axis=0)
        u2 = jnp.concatenate([h2, r2[SUBLANES:]], axis=0)
        carry_ref[:, col:col + tf] = u[tm - SUBLANES:tm, :]
        cw = cw_ref[:, col:col + tf]
        return cb_ref[:, col:col + tf] + u2 * cw[0:1, :] + u1 * cw[1:2, :] + u * cw[2:3, :]

    tf = FFN_COLS
    tiles = list(range(0, D_FF, tf))

    def up(n):
        j, slot = tiles[n], n % 2
        ubuf_ref[slot, :, 0:tf] = jnp.dot(h, wup_ref[:, j:j + tf], preferred_element_type=F32)
        ubuf_ref[slot, :, tf:2 * tf] = jnp.dot(h, wup_ref[:, D_FF + j:D_FF + j + tf], preferred_element_type=F32)

    def glu(n):
        j, slot = tiles[n], n % 2
        val = conv(ubuf_ref[slot, :, 0:tf], j)
        gate = conv(ubuf_ref[slot, :, tf:2 * tf], D_FF + j)
        abuf_ref[:, j:j + tf] = (gate * _sigmoid(gate) * val).astype(BF16)

    up(0)
    for n in range(1, len(tiles)):
        up(n)
        glu(n - 1)
    glu(len(tiles) - 1)
    y = jnp.dot(abuf_ref[...], wdn_ref[...], preferred_element_type=F32)
    x2 = x1 + gtf_ref[0] * y
    o_ref[0] = _rms(x2) * gfin_ref[...]


def _ffn(x, y_rwkv, y_ret, gt_a, sh_f, sc_f, gt_f, w_out, g_ffn, w_up, conv_w, conv_b, w_down, g_fin):
    B, S, D = x.shape
    tm = FFN_ROWS
    row = lambda b, t: (b, t, 0)
    per_b = lambda b, t: (b, 0, 0)
    const = lambda b, t: (0, 0)
    once = lambda a: pl.BlockSpec(a.shape, const, pipeline_mode=pl.Buffered(1))
    vecs = [g_ffn.reshape(1, D)]
    return pl.pallas_call(
        _ffn_kernel,
        grid=(B, S // tm),
        in_specs=[pl.BlockSpec((1, tm, D), row),
                  pl.BlockSpec((1, tm, RWKV_WIDTH), row),
                  pl.BlockSpec((1, tm, RET_WIDTH), row),
                  pl.BlockSpec((1, 1, D), per_b), pl.BlockSpec((1, 1, D), per_b),
                  pl.BlockSpec((1, 1, D), per_b), pl.BlockSpec((1, 1, D), per_b),
                  once(w_out), once(vecs[0]), once(w_up), once(conv_w), pl.BlockSpec((1, 2 * D_FF), const),
                  once(w_down), pl.BlockSpec((1, D), const)],
        out_specs=pl.BlockSpec((1, tm, D), row),
        out_shape=jax.ShapeDtypeStruct((B, S, D), F32),
        scratch_shapes=[pltpu.VMEM((SUBLANES, 2 * D_FF), F32),
                        pltpu.VMEM((2, tm, 2 * FFN_COLS), F32),
                        pltpu.VMEM((tm, D_FF), BF16)],
        compiler_params=pltpu.CompilerParams(
            dimension_semantics=("parallel", "arbitrary"), vmem_limit_bytes=VMEM_LIMIT_BYTES),
        name="ffn",
    )(x, y_rwkv, y_ret, gt_a, sh_f, sc_f, gt_f, w_out, vecs[0], w_up, conv_w,
      conv_b.reshape(1, 2 * D_FF), w_down, g_fin.reshape(1, D))


def _tables(S):
    pos = np.arange(S, dtype=np.float64)
    inv_freq = ROPE_BASE ** (-np.arange(0, HEAD_DIM, 2, dtype=np.float64) / HEAD_DIM)
    ang = pos[:, None] * inv_freq[None, :]
    cos, sin = np.cos(ang), np.sin(ang)
    cos2 = np.concatenate([cos, cos, cos, cos], axis=1)
    sin2 = np.concatenate([-sin, sin, -sin, sin], axis=1)
    log_gamma = np.log1p(-(2.0 ** (-5.0 - np.arange(N_HEADS, dtype=np.float64))))
    idx = np.arange(CHUNK, dtype=np.float64)
    dmat = np.exp(log_gamma[:, None, None] * np.abs(idx[:, None] - idx[None, :]))
    q_dec = np.exp(log_gamma[:, None] * (idx + 1.0))
    k_dec = np.exp(log_gamma[:, None] * (CHUNK - 1.0 - idx))
    widen = lambda t: np.repeat(t.T, HEAD_DIM, axis=1)
    dmw = dmat.transpose(1, 0, 2).reshape(CHUNK, N_HEADS * CHUNK)
    gcw = np.repeat(np.exp(log_gamma * CHUNK), HEAD_DIM)[None, :]
    head = np.arange(GROUP) // HEAD_DIM
    mask = head[:, None] == head[None, :]
    f32 = lambda t: jnp.asarray(t, F32)
    return f32(cos2), f32(sin2), f32(dmw), f32(widen(q_dec)), f32(widen(k_dec)), f32(gcw), jnp.asarray(mask, BF16)


def kernel(x, c, w_ada, b_ada, attn_norm_g, w_in, rwkv_mu, rwkv_w0, rwkv_w2, rwkv_a0, rwkv_a2, rwkv_g2,
           rwkv_k_k, rwkv_k_a, rwkv_r_k, rwkv_ln_g, rwkv_ln_b, ret_gn_g, w_out, ffn_norm_g, ffn_w_up,
           ffn_conv_w, ffn_conv_b, ffn_w_down, final_norm_g):
    B, S, D = x.shape
    assert D == D_MODEL and S % MIXER_TOKENS == 0 and S % FFN_ROWS == 0 and w_ada.shape[0] == 1
    cos2, sin2, dmw, qdec, kdec, gcw, mask = _tables(S)
    mod = _ada(c, w_ada[0], b_ada[0])
    sh_a, sc_a, gt_a, sh_f, sc_f, gt_f = [m.reshape(B, 1, D) for m in jnp.split(mod, N_MOD, axis=-1)]
    y_rwkv, y_ret = _mixer(x, attn_norm_g[0], sc_a, sh_a, w_in[0].astype(BF16),
                           rwkv_mu[0], rwkv_w0[0], rwkv_w2[0], rwkv_a0[0], rwkv_a2[0], rwkv_g2[0],
                           rwkv_k_k[0], rwkv_k_a[0], rwkv_r_k[0], rwkv_ln_g[0], rwkv_ln_b[0],
                           cos2, sin2, dmw, qdec, kdec, gcw, ret_gn_g[0], mask)
    return _ffn(x, y_rwkv, y_ret, gt_a, sh_f, sc_f, gt_f, w_out[0].astype(BF16), ffn_norm_g[0],
                ffn_w_up[0].astype(BF16), ffn_conv_w[0], ffn_conv_b[0], ffn_w_down[0].astype(BF16),
                final_norm_g)
```

```python
import functools
import math

import jax
import jax.numpy as jnp
import numpy as np
from jax import lax
from jax.experimental import pallas as pl
from jax.experimental.pallas import tpu as pltpu

F32 = jnp.float32
BF16 = jnp.bfloat16

D_MODEL = 1024
CHUNK = 64
HEAD_DIM = 64
RWKV_WIDTH = 512
RET_WIDTH = 512
N_HEADS = 8
DECAY_LORA = 64
AAA_LORA = 64
GATE_LORA = 128
RWKV_COLS = 3 * RWKV_WIDTH + DECAY_LORA + AAA_LORA + GATE_LORA
RET_COLS = 4 * RET_WIDTH
D_FF = 2816
ROPE_BASE = 10000.0
NORM_EPS = 1e-6
RWKV_LN_EPS = 64e-5
RET_GN_EPS = 1e-6
W_DECAY_SCALE = math.exp(-0.5)
N_MOD = 6
MXU_DIM = 256
SUBLANES = 8
GROUP = MXU_DIM
HEADS_PER_GROUP = GROUP // HEAD_DIM
KK_NORM_FLOOR = 1e-12

VMEM_LIMIT_BYTES = 56 * 1024 * 1024
MIXER_TOKENS = 512
FFN_ROWS = 512
FFN_COLS = MXU_DIM
INPROJ_COLS = MXU_DIM
ADA_COLS = 1536

_NT = (((1,), (1,)), ((), ()))


def _bdot(a, b):
    return jnp.dot(a.astype(BF16), b.astype(BF16), preferred_element_type=F32)


def _split_parts(x, parts=3):
    out, rem = [], x
    for i in range(parts):
        p = rem.astype(BF16)
        out.append(p)
        if i + 1 < parts:
            rem = rem - p.astype(F32)
    return out


def _head_sums(x, mask, parts=2):
    outs = []
    for g in range(x.shape[1] // GROUP):
        acc = None
        for p in _split_parts(x[:, g * GROUP:(g + 1) * GROUP], parts=parts):
            d = jnp.dot(p, mask, preferred_element_type=F32)
            acc = d if acc is None else acc + d
        outs.append(acc)
    return jnp.concatenate(outs, axis=1)


def _exact_dot_left(m_bf16, x):
    acc = None
    for p in _split_parts(x):
        d = jnp.dot(m_bf16, p, preferred_element_type=F32)
        acc = d if acc is None else acc + d
    return acc


def _sigmoid(x):
    return 1.0 / (1.0 + jnp.exp(-x))


def _rms(x):
    return x * lax.rsqrt(jnp.mean(x * x, axis=-1, keepdims=True) + NORM_EPS)


def _ada_kernel(c_ref, w_ref, b_ref, o_ref):
    cv = c_ref[...]
    s = cv * _sigmoid(cv)
    o_ref[...] = jnp.dot(s, w_ref[...], preferred_element_type=F32,
                         precision=lax.Precision.HIGHEST) + b_ref[...]


def _ada(c, w, b):
    B = c.shape[0]
    n = w.shape[1]
    return pl.pallas_call(
        _ada_kernel,
        grid=(n // ADA_COLS,),
        in_specs=[pl.BlockSpec((B, D_MODEL), lambda j: (0, 0)),
                  pl.BlockSpec((D_MODEL, ADA_COLS), lambda j: (0, j)),
                  pl.BlockSpec((1, ADA_COLS), lambda j: (0, j))],
        out_specs=pl.BlockSpec((B, ADA_COLS), lambda j: (0, j)),
        out_shape=jax.ShapeDtypeStruct((B, n), F32),
        compiler_params=pltpu.CompilerParams(vmem_limit_bytes=VMEM_LIMIT_BYTES),
        name="ada",
    )(c, w, b.reshape(1, n))


def _head_norm(y, mask, eps):
    mean = _head_sums(y, mask) * (1.0 / HEAD_DIM)
    d = y - mean
    var = _head_sums(d * d, mask, parts=1) * (1.0 / HEAD_DIM)
    return d * lax.rsqrt(var + eps)


def _shift_rows(x, first):
    rolled = pltpu.roll(x, 1, 0)
    head = jnp.where(lax.broadcasted_iota(jnp.int32, (SUBLANES, 1), 0) == 0, first, rolled[0:SUBLANES])
    return jnp.concatenate([head, rolled[SUBLANES:]], axis=0)


def _bd(x, mask):
    xb = x.astype(BF16)
    half = GROUP // 2
    zeros = jnp.zeros((HEAD_DIM, half), BF16)
    blocks = []
    for h in range(HEADS_PER_GROUP):
        rows = slice(h * HEAD_DIM, (h + 1) * HEAD_DIM)
        if h < HEADS_PER_GROUP // 2:
            blocks.append(jnp.concatenate([xb[:, 0:half] * mask[rows, 0:half], zeros], axis=1))
        else:
            blocks.append(jnp.concatenate([zeros, xb[:, half:GROUP] * mask[rows, half:GROUP]], axis=1))
    return jnp.concatenate(blocks, axis=0)


def _block_transpose(x):
    r = jnp.concatenate([x] * HEADS_PER_GROUP, axis=0).T
    blk = lax.broadcasted_iota(jnp.int32, (1, GROUP), 1) // HEAD_DIM
    out = r[0:HEAD_DIM]
    for h in range(1, HEADS_PER_GROUP):
        out = jnp.where(blk == h, r[h * HEAD_DIM:(h + 1) * HEAD_DIM], out)
    return out


def _wdot(lhs, rhs_bd):
    return jnp.dot(lhs.astype(BF16), rhs_bd, preferred_element_type=F32)


def _wdot_nt(lhs, rhs_bd):
    return lax.dot_general(lhs.astype(BF16), rhs_bd, _NT, preferred_element_type=F32)


def _interleave(*gens):
    results = [None] * len(gens)
    live = list(range(len(gens)))
    while live:
        for i in list(live):
            try:
                next(gens[i])
            except StopIteration as stop:
                results[i] = stop.value
                live.remove(i)
    return results


def _rwkv_prep(z, zfirst, prm, mask):
    W = RWKV_WIDTH
    zprev = _shift_rows(z, zfirst)
    zm = z + prm["mu"] * (zprev - z)
    r = zm[:, 0:W]
    k = zm[:, W:2 * W]
    v = zm[:, 2 * W:3 * W]
    wd = zm[:, 3 * W:3 * W + DECAY_LORA]
    ad = zm[:, 3 * W + DECAY_LORA:3 * W + DECAY_LORA + AAA_LORA]
    gd = zm[:, 3 * W + DECAY_LORA + AAA_LORA:RWKV_COLS]

    lw = -W_DECAY_SCALE * _sigmoid(prm["w0"] + _bdot(jnp.tanh(wd), prm["w2"]))
    a = _sigmoid(prm["a0"] + _bdot(ad, prm["a2"]))
    g = _bdot(_sigmoid(gd), prm["g2"])
    kkr = k * prm["k_k"]
    yield
    kkn = kkr * lax.rsqrt(jnp.maximum(_head_sums(kkr * kkr, mask, parts=1), KK_NORM_FLOOR ** 2))
    km = k * (1.0 + (a - 1.0) * prm["k_a"])
    yield
    bonus = _head_sums(r * km * prm["r_k"], mask) * v
    return dict(r=r, v=v, lw=lw, kkn=kkn, km=km, kb=kkn * a, bonus=bonus, g=g)


def _rwkv_chunk_operands(head, c):
    C = CHUNK
    rows = slice(c * C, (c + 1) * C)
    ri = lax.broadcasted_iota(jnp.int32, (C, C), 0)
    ci = lax.broadcasted_iota(jnp.int32, (C, C), 1)
    lw_c = head["lw"][rows]
    lp = _exact_dot_left((ri >= ci).astype(BF16), lw_c)
    em = jnp.exp(-lp)
    lpc = lp[C - 1:C, :]
    ee = jnp.exp(lpc - lp)
    kb, km = head["kb"][rows], head["km"][rows]
    fields = dict(rt=head["r"][rows] * jnp.exp(lp), at=-head["kkn"][rows] * jnp.exp(lp - lw_c),
                  bt=kb * em, kt=km * em, b_d=kb * ee, k_d=km * ee, v=head["v"][rows])
    return fields, lpc


def _rwkv_solve(get, hcur, mask, n_chunks):
    C = CHUNK
    n_groups = RWKV_WIDTH // GROUP
    rw = lax.broadcasted_iota(jnp.int32, (C, GROUP), 0)
    cw = lax.broadcasted_iota(jnp.int32, (C, GROUP), 1) % HEAD_DIM
    strict = rw > cw
    incl = rw >= cw
    eye = (rw == cw).astype(F32)

    def scores(u):
        lhs = jnp.concatenate([u["at"], u["rt"]], axis=0)
        sb = _wdot_nt(lhs, _bd(u["bt"], mask))
        sk = _wdot_nt(lhs, _bd(u["kt"], mask))
        u["p"] = jnp.where(strict, sb[0:C], 0.0)
        u["m_rb"] = jnp.where(incl, sb[C:2 * C], 0.0)
        u["l_ak"] = jnp.where(strict, sk[0:C], 0.0)
        u["m_rk"] = jnp.where(incl, sk[C:2 * C], 0.0)
        u["tinv"] = eye + u["p"]
        u["bt_b"] = _block_transpose(u["b_d"])
        u["bt_k"] = _block_transpose(u["k_d"])
        u["pcw"] = _block_transpose(u["pc"])

    def square(u):
        u["p"] = _wdot(u["p"], _bd(u["p"], mask))

    def double(u):
        both = _wdot(jnp.concatenate([u["p"], u["tinv"]], axis=0), _bd(u["p"], mask))
        u["p"] = both[0:C]
        u["tinv"] = u["tinv"] + both[C:2 * C]

    def double_last(u):
        u["tinv"] = u["tinv"] + _wdot(u["tinv"], _bd(u["p"], mask))

    def with_v(u):
        rv = _wdot(jnp.concatenate([u["l_ak"], u["m_rk"], u["bt_k"]], axis=0), _bd(u["v"], mask))
        u["x"], u["y0"], u["psi"] = rv[0:C], rv[C:2 * C], rv[2 * C:3 * C]

    def times_t(u):
        u["gt"] = _wdot(jnp.concatenate([u["bt_b"], u["m_rb"]], axis=0), _bd(u["tinv"], mask))

    def fold(u):
        rw_ = _wdot(u["gt"], _bd(u["at"], mask))
        ru = _wdot(u["gt"], _bd(u["x"], mask))
        u["po"] = jnp.concatenate([rw_[0:C], u["rt"] + rw_[C:2 * C]], axis=0)
        u["psi"] = u["psi"] + ru[0:C]
        u["y0"] = u["y0"] + ru[C:2 * C]

    hcur = list(hcur)
    ys = [[None] * n_groups for _ in range(n_chunks)]

    def carry(u):
        c, gi = u["c"], u["g"]
        ph = _wdot(u["po"], _bd(hcur[gi], mask))
        ys[c][gi] = ph[C:2 * C] + u["y0"]
        hcur[gi] = u["pcw"] * hcur[gi] + ph[0:C] + u["psi"]

    stages = [scores, square, double, double, double, double, double_last, with_v, times_t, fold, carry]

    units = {}
    for t in range(n_chunks + len(stages)):
        for c in range(n_chunks):
            st = t - c - 1
            if st == -1:
                fields, lpc = get(c)
                pc = jnp.broadcast_to(jnp.exp(lpc), (C, RWKV_WIDTH))
                units[c] = []
                for gi in range(n_groups):
                    ln = slice(gi * GROUP, (gi + 1) * GROUP)
                    u = {name: val[:, ln] for name, val in fields.items()}
                    u.update(c=c, g=gi, pc=pc[:, ln])
                    units[c].append(u)
            elif 0 <= st < len(stages):
                for u in units[c]:
                    stages[st](u)
        yield
    y = jnp.concatenate([jnp.concatenate(ys[c], axis=1) for c in range(n_chunks)], axis=0)
    return y, hcur


def _ret_mix(z_ref, cos_ref, sin_ref, dm_ref, qdec_ref, kdec_ref, gc_ref, scur, mask):
    C = CHUNK
    W = RET_WIDTH
    T = z_ref.shape[0]
    n_chunks = T // C
    n_groups = W // GROUP
    reps = W // cos_ref.shape[1]
    cosf = jnp.concatenate([cos_ref[...]] * reps, axis=1)
    sinf = jnp.concatenate([sin_ref[...]] * reps, axis=1)
    lane = lax.broadcasted_iota(jnp.int32, (1, W), 1)
    first_half = (lane % HEAD_DIM) < (HEAD_DIM // 2)

    def rot(t):
        swapped = jnp.where(first_half, pltpu.roll(t, W - HEAD_DIM // 2, 1), pltpu.roll(t, HEAD_DIM // 2, 1))
        return t * cosf + swapped * sinf

    q = rot(z_ref[:, 0:W]) * (HEAD_DIM ** -0.5)
    yield
    k = rot(z_ref[:, W:2 * W])
    v = z_ref[:, 2 * W:3 * W]
    yield

    units = []
    for c in range(n_chunks):
        rows = slice(c * C, (c + 1) * C)
        qd = q[rows] * qdec_ref[...]
        kd = k[rows] * kdec_ref[...]
        for gi in range(n_groups):
            ln = slice(gi * GROUP, (gi + 1) * GROUP)
            units.append(dict(c=c, g=gi, q=q[rows, ln], k=k[rows, ln], v=v[rows, ln], qd=qd[:, ln], kd=kd[:, ln]))
    for u in units:
        u["sc"] = _wdot_nt(u["q"], _bd(u["k"], mask)) * dm_ref[:, u["g"] * GROUP:(u["g"] + 1) * GROUP]
        u["kdt"] = _block_transpose(u["kd"])
    yield
    for u in units:
        both = _wdot(jnp.concatenate([u["sc"], u["kdt"]], axis=0), _bd(u["v"], mask))
        u["y"], u["kv"] = both[0:C], both[C:2 * C]
    yield

    scur = list(scur)
    ys = [[None] * n_groups for _ in range(n_chunks)]
    for u in units:
        c, gi = u["c"], u["g"]
        ys[c][gi] = u["y"] + _wdot(u["qd"], _bd(scur[gi], mask))
        scur[gi] = scur[gi] * gc_ref[:, gi * GROUP:(gi + 1) * GROUP] + u["kv"]
    y = jnp.concatenate([jnp.concatenate(ys[c], axis=1) for c in range(n_chunks)], axis=0)
    return y, scur


def _mixer_kernel(x_ref, g_ref, sc_ref, sh_ref, win_ref,
                  mu_ref, w0_ref, w2_ref, a0_ref, a2_ref, g2_ref, kk_ref, ka_ref, rk_ref, lng_ref, lnb_ref,
                  cos_ref, sin_ref, dm_ref, qdec_ref, kdec_ref, gc_ref, gng_ref, mask_ref,
                  yr_ref, yt_ref, zr_ref, zt_ref, zlast_ref, rstate_ref, tstate_ref):
    T = x_ref.shape[1]
    n_groups = RWKV_WIDTH // GROUP

    @pl.when(pl.program_id(1) == 0)
    def _():
        zlast_ref[...] = jnp.zeros_like(zlast_ref)
        rstate_ref[...] = jnp.zeros_like(rstate_ref)
        tstate_ref[...] = jnp.zeros_like(tstate_ref)

    h = (_rms(x_ref[0]) * g_ref[...] * (1.0 + sc_ref[0]) + sh_ref[0]).astype(BF16)
    step = INPROJ_COLS
    for j in range(0, RWKV_COLS, step):
        zr_ref[:, j:j + step] = jnp.dot(h, win_ref[:, j:j + step], preferred_element_type=F32)

    def ret_proj():
        for j in range(0, RET_COLS, step):
            zt_ref[:, j:j + step] = jnp.dot(h, win_ref[:, RWKV_COLS + j:RWKV_COLS + j + step],
                                            preferred_element_type=F32)
            yield

    mask = mask_ref[...]
    prm = dict(mu=mu_ref[...], w0=w0_ref[...], w2=w2_ref[...], a0=a0_ref[...], a2=a2_ref[...],
               g2=g2_ref[...], k_k=kk_ref[...], k_a=ka_ref[...], r_k=rk_ref[...])
    head, _ = _interleave(_rwkv_prep(zr_ref[...], zlast_ref[0:1, :], prm, mask), ret_proj())
    zlast_ref[0:1, :] = zr_ref[T - 1:T, :]

    (y, hcur), (yt, scur) = _interleave(
        _rwkv_solve(functools.partial(_rwkv_chunk_operands, head), [rstate_ref[gi] for gi in range(n_groups)],
                    mask, T // CHUNK),
        _ret_mix(zt_ref, cos_ref, sin_ref, dm_ref, qdec_ref, kdec_ref, gc_ref,
                 [tstate_ref[gi] for gi in range(n_groups)], mask))
    for gi in range(n_groups):
        rstate_ref[gi] = hcur[gi]
        tstate_ref[gi] = scur[gi]
    y = _head_norm(y, mask, RWKV_LN_EPS) * lng_ref[...] + lnb_ref[...]
    yr_ref[0] = ((y + head["bonus"]) * head["g"]).astype(BF16)
    gate = zt_ref[:, 3 * RET_WIDTH:4 * RET_WIDTH]
    yt_ref[0] = (gate * _sigmoid(gate) * (_head_norm(yt, mask, RET_GN_EPS) * gng_ref[...])).astype(BF16)


def _mixer(x, g, sc, sh, w_in, mu, w0, w2, a0, a2, g2, k_k, k_a, r_k, ln_g, ln_b,
           cos2, sin2, dmw, qdec, kdec, gcw, gn_g, mask):
    B, S, D = x.shape
    T = MIXER_TOKENS
    W = RWKV_WIDTH
    row = lambda b, t: (b, t, 0)
    per_b = lambda b, t: (b, 0, 0)
    vec = lambda a: a.reshape(1, -1)
    full = lambda a: pl.BlockSpec(a.shape, lambda b, t: (0, 0))
    tab = lambda a: pl.BlockSpec((T, a.shape[1]), lambda b, t: (t, 0))
    params = [vec(mu), vec(w0), w2.astype(BF16), vec(a0), a2.astype(BF16), g2.astype(BF16),
              vec(k_k), vec(k_a), vec(r_k), vec(ln_g), vec(ln_b)]
    tables = [dmw, qdec, kdec, gcw, vec(gn_g), mask]
    return pl.pallas_call(
        _mixer_kernel,
        grid=(B, S // T),
        in_specs=[pl.BlockSpec((1, T, D), row), full(vec(g)), pl.BlockSpec((1, 1, D), per_b),
                  pl.BlockSpec((1, 1, D), per_b), full(w_in)]
                 + [full(a) for a in params] + [tab(cos2), tab(sin2)] + [full(a) for a in tables],
        out_specs=[pl.BlockSpec((1, T, W), row), pl.BlockSpec((1, T, RET_WIDTH), row)],
        out_shape=[jax.ShapeDtypeStruct((B, S, W), BF16), jax.ShapeDtypeStruct((B, S, RET_WIDTH), BF16)],
        scratch_shapes=[pltpu.VMEM((T, RWKV_COLS), F32),
                        pltpu.VMEM((T, RET_COLS), F32),
                        pltpu.VMEM((SUBLANES, RWKV_COLS), F32),
                        pltpu.VMEM((W // GROUP, HEAD_DIM, GROUP), F32),
                        pltpu.VMEM((RET_WIDTH // GROUP, HEAD_DIM, GROUP), F32)],
        compiler_params=pltpu.CompilerParams(
            dimension_semantics=("parallel", "arbitrary"), vmem_limit_bytes=VMEM_LIMIT_BYTES),
        name="mixer",
    )(x, vec(g), sc, sh, w_in, *params, cos2, sin2, *tables)


def _ffn_kernel(x_ref, yr_ref, yt_ref, gta_ref, shf_ref, scf_ref, gtf_ref, wout_ref, gnf_ref,
                wup_ref, cw_ref, cb_ref, wdn_ref, gfin_ref, o_ref, carry_ref, ubuf_ref, abuf_ref):
    tm = x_ref.shape[1]

    @pl.when(pl.program_id(1) == 0)
    def _():
        carry_ref[...] = jnp.zeros_like(carry_ref)

    ymix = (jnp.dot(yr_ref[0], wout_ref[0:RWKV_WIDTH, :], preferred_element_type=F32)
            + jnp.dot(yt_ref[0], wout_ref[RWKV_WIDTH:, :], preferred_element_type=F32))
    x1 = x_ref[0] + gta_ref[0] * ymix
    q_rows = tm // SUBLANES

    def strided(t):
        return jnp.swapaxes(t.reshape(SUBLANES, q_rows, t.shape[1]), 0, 1).reshape(tm, t.shape[1])

    def unstrided(t):
        return jnp.swapaxes(t.reshape(q_rows, SUBLANES, t.shape[1]), 0, 1).reshape(tm, t.shape[1])

    h = strided(_rms(x1) * gnf_ref[...] * (1.0 + scf_ref[0]) + shf_ref[0]).astype(BF16)
    row8 = lax.broadcasted_iota(jnp.int32, (SUBLANES, 1), 0)

    def conv(u, col):
        tf = u.shape[1]
        prev = carry_ref[:, col:col + tf]
        last, before = u[tm - SUBLANES:tm], u[tm - 2 * SUBLANES:tm - SUBLANES]
        d1 = jnp.where(row8 == 0, prev[1:2, :], pltpu.roll(last, 1, 0))
        d2 = jnp.where(row8 == 0, prev[0:1, :], pltpu.roll(before, 1, 0))
        u1 = jnp.concatenate([d1, u[0:tm - SUBLANES]], axis=0)
        u2 = jnp.concatenate([d2, d1, u[0:tm - 2 * SUBLANES]], axis=0)
        carry_ref[0:1, col:col + tf] = before[SUBLANES - 1:SUBLANES, :]
        carry_ref[1:2, col:col + tf] = last[SUBLANES - 1:SUBLANES, :]
        cw = cw_ref[:, col:col + tf]
        return cb_ref[:, col:col + tf] + u2 * cw[0:1, :] + u1 * cw[1:2, :] + u * cw[2:3, :]

    tf = FFN_COLS
    tiles = list(range(0, D_FF, tf))

    def up(n):
        j, slot = tiles[n], n % 2
        ubuf_ref[slot, :, 0:tf] = jnp.dot(h, wup_ref[:, j:j + tf], preferred_element_type=F32)
        ubuf_ref[slot, :, tf:2 * tf] = jnp.dot(h, wup_ref[:, D_FF + j:D_FF + j + tf], preferred_element_type=F32)

    def glu(n):
        j, slot = tiles[n], n % 2
        val = conv(ubuf_ref[slot, :, 0:tf], j)
        gate = conv(ubuf_ref[slot, :, tf:2 * tf], D_FF + j)
        abuf_ref[:, j:j + tf] = (gate * _sigmoid(gate) * val).astype(BF16)

    up(0)
    for n in range(1, len(tiles)):
        up(n)
        glu(n - 1)
    glu(len(tiles) - 1)
    y = unstrided(jnp.dot(abuf_ref[...], wdn_ref[...], preferred_element_type=F32))
    x2 = x1 + gtf_ref[0] * y
    o_ref[0] = _rms(x2) * gfin_ref[...]


def _ffn(x, y_rwkv, y_ret, gt_a, sh_f, sc_f, gt_f, w_out, g_ffn, w_up, conv_w, conv_b, w_down, g_fin):
    B, S, D = x.shape
    tm = FFN_ROWS
    row = lambda b, t: (b, t, 0)
    per_b = lambda b, t: (b, 0, 0)
    const = lambda b, t: (0, 0)
    once = lambda a: pl.BlockSpec(a.shape, const, pipeline_mode=pl.Buffered(1))
    vecs = [g_ffn.reshape(1, D)]
    return pl.pallas_call(
        _ffn_kernel,
        grid=(B, S // tm),
        in_specs=[pl.BlockSpec((1, tm, D), row),
                  pl.BlockSpec((1, tm, RWKV_WIDTH), row),
                  pl.BlockSpec((1, tm, RET_WIDTH), row),
                  pl.BlockSpec((1, 1, D), per_b), pl.BlockSpec((1, 1, D), per_b),
                  pl.BlockSpec((1, 1, D), per_b), pl.BlockSpec((1, 1, D), per_b),
                  once(w_out), once(vecs[0]), once(w_up), once(conv_w), pl.BlockSpec((1, 2 * D_FF), const),
                  once(w_down), pl.BlockSpec((1, D), const)],
        out_specs=pl.BlockSpec((1, tm, D), row),
        out_shape=jax.ShapeDtypeStruct((B, S, D), F32),
        scratch_shapes=[pltpu.VMEM((SUBLANES, 2 * D_FF), F32),
                        pltpu.VMEM((2, tm, 2 * FFN_COLS), F32),
                        pltpu.VMEM((tm, D_FF), BF16)],
        compiler_params=pltpu.CompilerParams(
            dimension_semantics=("parallel", "arbitrary"), vmem_limit_bytes=VMEM_LIMIT_BYTES),
        name="ffn",
    )(x, y_rwkv, y_ret, gt_a, sh_f, sc_f, gt_f, w_out, vecs[0], w_up, conv_w,
      conv_b.reshape(1, 2 * D_FF), w_down, g_fin.reshape(1, D))


def _tables(S):
    pos = np.arange(S, dtype=np.float64)
    inv_freq = ROPE_BASE ** (-np.arange(0, HEAD_DIM, 2, dtype=np.float64) / HEAD_DIM)
    ang = pos[:, None] * inv_freq[None, :]
    cos, sin = np.cos(ang), np.sin(ang)
    cos2 = np.concatenate([cos, cos, cos, cos], axis=1)
    sin2 = np.concatenate([-sin, sin, -sin, sin], axis=1)
    log_gamma = np.log1p(-(2.0 ** (-5.0 - np.arange(N_HEADS, dtype=np.float64))))
    idx = np.arange(CHUNK, dtype=np.float64)
    dmat = np.exp(log_gamma[:, None, None] * np.abs(idx[:, None] - idx[None, :]))
    q_dec = np.exp(log_gamma[:, None] * (idx + 1.0))
    k_dec = np.exp(log_gamma[:, None] * (CHUNK - 1.0 - idx))
    widen = lambda t: np.repeat(t.T, HEAD_DIM, axis=1)
    dmw = dmat.transpose(1, 0, 2).reshape(CHUNK, N_HEADS * CHUNK)
    gcw = np.repeat(np.exp(log_gamma * CHUNK), HEAD_DIM)[None, :]
    head = np.arange(GROUP) // HEAD_DIM
    mask = head[:, None] == head[None, :]
    f32 = lambda t: jnp.asarray(t, F32)
    return f32(cos2), f32(sin2), f32(dmw), f32(widen(q_dec)), f32(widen(k_dec)), f32(gcw), jnp.asarray(mask, BF16)


def kernel(x, c, w_ada, b_ada, attn_norm_g, w_in, rwkv_mu, rwkv_w0, rwkv_w2, rwkv_a0, rwkv_a2, rwkv_g2,
           rwkv_k_k, rwkv_k_a, rwkv_r_k, rwkv_ln_g, rwkv_ln_b, ret_gn_g, w_out, ffn_norm_g, ffn_w_up,
           ffn_conv_w, ffn_conv_b, ffn_w_down, final_norm_g):
    B, S, D = x.shape
    assert D == D_MODEL and S % MIXER_TOKENS == 0 and S % FFN_ROWS == 0 and w_ada.shape[0] == 1
    cos2, sin2, dmw, qdec, kdec, gcw, mask = _tables(S)
    mod = _ada(c, w_ada[0], b_ada[0])
    sh_a, sc_a, gt_a, sh_f, sc_f, gt_f = [m.reshape(B, 1, D) for m in jnp.split(mod, N_MOD, axis=-1)]
    y_rwkv, y_ret = _mixer(x, attn_norm_g[0], sc_a, sh_a, w_in[0].astype(BF16),
                           rwkv_mu[0], rwkv_w0[0], rwkv_w2[0], rwkv_a0[0], rwkv_a2[0], rwkv_g2[0],
                           rwkv_k_k[0], rwkv_k_a[0], rwkv_r_k[0], rwkv_ln_g[0], rwkv_ln_b[0],
                           cos2, sin2, dmw, qdec, kdec, gcw, ret_gn_g[0], mask)
    return _ffn(x, y_rwkv, y_ret, gt_a, sh_f, sc_f, gt_f, w_out[0].astype(BF16), ffn_norm_g[0],
                ffn_w_up[0].astype(BF16), ffn_conv_w[0], ffn_conv_b[0], ffn_w_down[0].astype(BF16),
                final_norm_g)
```

```python
import functools
import math

import jax
import jax.numpy as jnp
import numpy as np
from jax import lax
from jax.experimental import pallas as pl
from jax.experimental.pallas import tpu as pltpu

F32 = jnp.float32
BF16 = jnp.bfloat16

D_MODEL = 1024
CHUNK = 64
HEAD_DIM = 64
RWKV_WIDTH = 512
RET_WIDTH = 512
N_HEADS = 8
DECAY_LORA = 64
AAA_LORA = 64
GATE_LORA = 128
RWKV_COLS = 3 * RWKV_WIDTH + DECAY_LORA + AAA_LORA + GATE_LORA
RET_COLS = 4 * RET_WIDTH
D_FF = 2816
ROPE_BASE = 10000.0
NORM_EPS = 1e-6
RWKV_LN_EPS = 64e-5
RET_GN_EPS = 1e-6
W_DECAY_SCALE = math.exp(-0.5)
N_MOD = 6
MXU_DIM = 256
SUBLANES = 8
GROUP = MXU_DIM
HEADS_PER_GROUP = GROUP // HEAD_DIM
KK_NORM_FLOOR = 1e-12

VMEM_LIMIT_BYTES = 56 * 1024 * 1024
MIXER_TOKENS = 512
FFN_ROWS = 512
FFN_COLS = MXU_DIM
INPROJ_COLS = MXU_DIM
ADA_COLS = 1536

_NT = (((1,), (1,)), ((), ()))


def _bdot(a, b):
    return jnp.dot(a.astype(BF16), b.astype(BF16), preferred_element_type=F32)


def _split_parts(x, parts=3):
    out, rem = [], x
    for i in range(parts):
        p = rem.astype(BF16)
        out.append(p)
        if i + 1 < parts:
            rem = rem - p.astype(F32)
    return out


def _head_sums(x, mask, parts=2):
    outs = []
    for g in range(x.shape[1] // GROUP):
        acc = None
        for p in _split_parts(x[:, g * GROUP:(g + 1) * GROUP], parts=parts):
            d = jnp.dot(p, mask, preferred_element_type=F32)
            acc = d if acc is None else acc + d
        outs.append(acc)
    return jnp.concatenate(outs, axis=1)


def _exact_dot_left(m_bf16, x):
    acc = None
    for p in _split_parts(x):
        d = jnp.dot(m_bf16, p, preferred_element_type=F32)
        acc = d if acc is None else acc + d
    return acc


def _sigmoid(x):
    return 1.0 / (1.0 + jnp.exp(-x))


def _rms(x):
    return x * lax.rsqrt(jnp.mean(x * x, axis=-1, keepdims=True) + NORM_EPS)


def _ada_kernel(c_ref, w_ref, b_ref, o_ref):
    cv = c_ref[...]
    s = cv * _sigmoid(cv)
    o_ref[...] = jnp.dot(s, w_ref[...], preferred_element_type=F32,
                         precision=lax.Precision.HIGHEST) + b_ref[...]


def _ada(c, w, b):
    B = c.shape[0]
    n = w.shape[1]
    return pl.pallas_call(
        _ada_kernel,
        grid=(n // ADA_COLS,),
        in_specs=[pl.BlockSpec((B, D_MODEL), lambda j: (0, 0)),
                  pl.BlockSpec((D_MODEL, ADA_COLS), lambda j: (0, j)),
                  pl.BlockSpec((1, ADA_COLS), lambda j: (0, j))],
        out_specs=pl.BlockSpec((B, ADA_COLS), lambda j: (0, j)),
        out_shape=jax.ShapeDtypeStruct((B, n), F32),
        compiler_params=pltpu.CompilerParams(vmem_limit_bytes=VMEM_LIMIT_BYTES),
        name="ada",
    )(c, w, b.reshape(1, n))


def _head_norm(y, mask, eps):
    mean = _head_sums(y, mask) * (1.0 / HEAD_DIM)
    d = y - mean
    var = _head_sums(d * d, mask, parts=1) * (1.0 / HEAD_DIM)
    return d * lax.rsqrt(var + eps)


def _shift_rows(x, first):
    rolled = pltpu.roll(x, 1, 0)
    head = jnp.where(lax.broadcasted_iota(jnp.int32, (SUBLANES, 1), 0) == 0, first, rolled[0:SUBLANES])
    return jnp.concatenate([head, rolled[SUBLANES:]], axis=0)


def _bd(x, mask):
    xb = x.astype(BF16)
    half = GROUP // 2
    zeros = jnp.zeros((HEAD_DIM, half), BF16)
    blocks = []
    for h in range(HEADS_PER_GROUP):
        rows = slice(h * HEAD_DIM, (h + 1) * HEAD_DIM)
        if h < HEADS_PER_GROUP // 2:
            blocks.append(jnp.concatenate([xb[:, 0:half] * mask[rows, 0:half], zeros], axis=1))
        else:
            blocks.append(jnp.concatenate([zeros, xb[:, half:GROUP] * mask[rows, half:GROUP]], axis=1))
    return jnp.concatenate(blocks, axis=0)


def _block_transpose(x):
    r = jnp.concatenate([x] * HEADS_PER_GROUP, axis=0).T
    blk = lax.broadcasted_iota(jnp.int32, (1, GROUP), 1) // HEAD_DIM
    out = r[0:HEAD_DIM]
    for h in range(1, HEADS_PER_GROUP):
        out = jnp.where(blk == h, r[h * HEAD_DIM:(h + 1) * HEAD_DIM], out)
    return out


def _wdot(lhs, rhs_bd):
    return jnp.dot(lhs.astype(BF16), rhs_bd, preferred_element_type=F32)


def _wdot_nt(lhs, rhs_bd):
    return lax.dot_general(lhs.astype(BF16), rhs_bd, _NT, preferred_element_type=F32)


def _interleave(*gens):
    results = [None] * len(gens)
    live = list(range(len(gens)))
    while live:
        for i in list(live):
            try:
                next(gens[i])
            except StopIteration as stop:
                results[i] = stop.value
                live.remove(i)
    return results


def _rwkv_prep(z, zfirst, prm, mask):
    W = RWKV_WIDTH
    zprev = _shift_rows(z, zfirst)
    zm = z + prm["mu"] * (zprev - z)
    r = zm[:, 0:W]
    k = zm[:, W:2 * W]
    v = zm[:, 2 * W:3 * W]
    wd = zm[:, 3 * W:3 * W + DECAY_LORA]
    ad = zm[:, 3 * W + DECAY_LORA:3 * W + DECAY_LORA + AAA_LORA]
    gd = zm[:, 3 * W + DECAY_LORA + AAA_LORA:RWKV_COLS]

    lw = -W_DECAY_SCALE * _sigmoid(prm["w0"] + _bdot(jnp.tanh(wd), prm["w2"]))
    a = _sigmoid(prm["a0"] + _bdot(ad, prm["a2"]))
    g = _bdot(_sigmoid(gd), prm["g2"])
    kkr = k * prm["k_k"]
    yield
    kkn = kkr * lax.rsqrt(jnp.maximum(_head_sums(kkr * kkr, mask, parts=1), KK_NORM_FLOOR ** 2))
    km = k * (1.0 + (a - 1.0) * prm["k_a"])
    yield
    bonus = _head_sums(r * km * prm["r_k"], mask) * v
    return dict(r=r, v=v, lw=lw, kkn=kkn, km=km, kb=kkn * a, bonus=bonus, g=g)


def _rwkv_chunk_operands(head, c):
    C = CHUNK
    rows = slice(c * C, (c + 1) * C)
    ri = lax.broadcasted_iota(jnp.int32, (C, C), 0)
    ci = lax.broadcasted_iota(jnp.int32, (C, C), 1)
    lw_c = head["lw"][rows]
    lp = _exact_dot_left((ri >= ci).astype(BF16), lw_c)
    em = jnp.exp(-lp)
    lpc = lp[C - 1:C, :]
    ee = jnp.exp(lpc - lp)
    kb, km = head["kb"][rows], head["km"][rows]
    fields = dict(rt=head["r"][rows] * jnp.exp(lp), at=-head["kkn"][rows] * jnp.exp(lp - lw_c),
                  bt=kb * em, kt=km * em, b_d=kb * ee, k_d=km * ee, v=head["v"][rows])
    return fields, lpc


def _rwkv_solve(get, hcur, mask, n_chunks):
    C = CHUNK
    n_groups = RWKV_WIDTH // GROUP
    rw = lax.broadcasted_iota(jnp.int32, (C, GROUP), 0)
    cw = lax.broadcasted_iota(jnp.int32, (C, GROUP), 1) % HEAD_DIM
    strict = rw > cw
    incl = rw >= cw
    eye = (rw == cw).astype(F32)

    def scores(u):
        lhs = jnp.concatenate([u["at"], u["rt"]], axis=0)
        sb = _wdot_nt(lhs, _bd(u["bt"], mask))
        sk = _wdot_nt(lhs, _bd(u["kt"], mask))
        u["p"] = jnp.where(strict, sb[0:C], 0.0)
        u["m_rb"] = jnp.where(incl, sb[C:2 * C], 0.0)
        u["l_ak"] = jnp.where(strict, sk[0:C], 0.0)
        u["m_rk"] = jnp.where(incl, sk[C:2 * C], 0.0)
        u["tinv"] = eye + u["p"]
        u["bt_b"] = _block_transpose(u["b_d"])
        u["bt_k"] = _block_transpose(u["k_d"])
        u["pcw"] = _block_transpose(u["pc"])

    def square(u):
        u["p"] = _wdot(u["p"], _bd(u["p"], mask))

    def double(u):
        both = _wdot(jnp.concatenate([u["p"], u["tinv"]], axis=0), _bd(u["p"], mask))
        u["p"] = both[0:C]
        u["tinv"] = u["tinv"] + both[C:2 * C]

    def double_last(u):
        u["tinv"] = u["tinv"] + _wdot(u["tinv"], _bd(u["p"], mask))

    def with_v(u):
        rv = _wdot(jnp.concatenate([u["l_ak"], u["m_rk"], u["bt_k"]], axis=0), _bd(u["v"], mask))
        u["x"], u["y0"], u["psi"] = rv[0:C], rv[C:2 * C], rv[2 * C:3 * C]

    def times_t(u):
        u["gt"] = _wdot(jnp.concatenate([u["bt_b"], u["m_rb"]], axis=0), _bd(u["tinv"], mask))

    def fold(u):
        rw_ = _wdot(u["gt"], _bd(u["at"], mask))
        ru = _wdot(u["gt"], _bd(u["x"], mask))
        u["po"] = jnp.concatenate([rw_[0:C], u["rt"] + rw_[C:2 * C]], axis=0)
        u["psi"] = u["psi"] + ru[0:C]
        u["y0"] = u["y0"] + ru[C:2 * C]

    hcur = list(hcur)
    ys = [[None] * n_groups for _ in range(n_chunks)]

    def carry(u):
        c, gi = u["c"], u["g"]
        ph = _wdot(u["po"], _bd(hcur[gi], mask))
        ys[c][gi] = ph[C:2 * C] + u["y0"]
        hcur[gi] = u["pcw"] * hcur[gi] + ph[0:C] + u["psi"]

    stages = [scores, square, double, double, double, double, double_last, with_v, times_t, fold, carry]

    units = {}
    for t in range(n_chunks + len(stages)):
        for c in range(n_chunks):
            st = t - c - 1
            if st == -1:
                fields, lpc = get(c)
                pc = jnp.broadcast_to(jnp.exp(lpc), (C, RWKV_WIDTH))
                units[c] = []
                for gi in range(n_groups):
                    ln = slice(gi * GROUP, (gi + 1) * GROUP)
                    u = {name: val[:, ln] for name, val in fields.items()}
                    u.update(c=c, g=gi, pc=pc[:, ln])
                    units[c].append(u)
            elif 0 <= st < len(stages):
                for u in units[c]:
                    stages[st](u)
        yield
    y = jnp.concatenate([jnp.concatenate(ys[c], axis=1) for c in range(n_chunks)], axis=0)
    return y, hcur


def _ret_mix(z_ref, cos_ref, sin_ref, dm_ref, qdec_ref, kdec_ref, gc_ref, scur, mask):
    C = CHUNK
    W = RET_WIDTH
    T = z_ref.shape[0]
    n_chunks = T // C
    n_groups = W // GROUP
    reps = W // cos_ref.shape[1]
    cosf = jnp.concatenate([cos_ref[...]] * reps, axis=1)
    sinf = jnp.concatenate([sin_ref[...]] * reps, axis=1)
    lane = lax.broadcasted_iota(jnp.int32, (1, W), 1)
    first_half = (lane % HEAD_DIM) < (HEAD_DIM // 2)

    def rot(t):
        swapped = jnp.where(first_half, pltpu.roll(t, W - HEAD_DIM // 2, 1), pltpu.roll(t, HEAD_DIM // 2, 1))
        return t * cosf + swapped * sinf

    q = rot(z_ref[:, 0:W]) * (HEAD_DIM ** -0.5)
    yield
    k = rot(z_ref[:, W:2 * W])
    v = z_ref[:, 2 * W:3 * W]
    yield

    units = []
    for c in range(n_chunks):
        rows = slice(c * C, (c + 1) * C)
        qd = q[rows] * qdec_ref[...]
        kd = k[rows] * kdec_ref[...]
        for gi in range(n_groups):
            ln = slice(gi * GROUP, (gi + 1) * GROUP)
            units.append(dict(c=c, g=gi, q=q[rows, ln], k=k[rows, ln], v=v[rows, ln], qd=qd[:, ln], kd=kd[:, ln]))
    for u in units:
        u["sc"] = _wdot_nt(u["q"], _bd(u["k"], mask)) * dm_ref[:, u["g"] * GROUP:(u["g"] + 1) * GROUP]
        u["kdt"] = _block_transpose(u["kd"])
    yield
    for u in units:
        both = _wdot(jnp.concatenate([u["sc"], u["kdt"]], axis=0), _bd(u["v"], mask))
        u["y"], u["kv"] = both[0:C], both[C:2 * C]
    yield

    scur = list(scur)
    ys = [[None] * n_groups for _ in range(n_chunks)]
    for u in units:
        c, gi = u["c"], u["g"]
        ys[c][gi] = u["y"] + _wdot(u["qd"], _bd(scur[gi], mask))
        scur[gi] = scur[gi] * gc_ref[:, gi * GROUP:(gi + 1) * GROUP] + u["kv"]
    y = jnp.concatenate([jnp.concatenate(ys[c], axis=1) for c in range(n_chunks)], axis=0)
    return y, scur


def _mixer_kernel(x_ref, g_ref, sc_ref, sh_ref, win_ref,
                  mu_ref, w0_ref, w2_ref, a0_ref, a2_ref, g2_ref, kk_ref, ka_ref, rk_ref, lng_ref, lnb_ref,
                  cos_ref, sin_ref, dm_ref, qdec_ref, kdec_ref, gc_ref, gng_ref, mask_ref,
                  yr_ref, yt_ref, zr_ref, zt_ref, zlast_ref, rstate_ref, tstate_ref):
    T = x_ref.shape[1]
    n_groups = RWKV_WIDTH // GROUP

    @pl.when(pl.program_id(1) == 0)
    def _():
        zlast_ref[...] = jnp.zeros_like(zlast_ref)
        rstate_ref[...] = jnp.zeros_like(rstate_ref)
        tstate_ref[...] = jnp.zeros_like(tstate_ref)

    h = (_rms(x_ref[0]) * g_ref[...] * (1.0 + sc_ref[0]) + sh_ref[0]).astype(BF16)
    step = INPROJ_COLS
    for j in range(0, RWKV_COLS, step):
        zr_ref[:, j:j + step] = jnp.dot(h, win_ref[:, j:j + step], preferred_element_type=F32)

    def ret_proj():
        for j in range(0, RET_COLS, step):
            zt_ref[:, j:j + step] = jnp.dot(h, win_ref[:, RWKV_COLS + j:RWKV_COLS + j + step],
                                            preferred_element_type=F32)
            yield

    mask = mask_ref[...]
    prm = dict(mu=mu_ref[...], w0=w0_ref[...], w2=w2_ref[...], a0=a0_ref[...], a2=a2_ref[...],
               g2=g2_ref[...], k_k=kk_ref[...], k_a=ka_ref[...], r_k=rk_ref[...])
    _, head = _interleave(ret_proj(), _rwkv_prep(zr_ref[...], zlast_ref[0:1, :], prm, mask))
    zlast_ref[0:1, :] = zr_ref[T - 1:T, :]

    (yt, scur), (y, hcur) = _interleave(
        _ret_mix(zt_ref, cos_ref, sin_ref, dm_ref, qdec_ref, kdec_ref, gc_ref,
                 [tstate_ref[gi] for gi in range(n_groups)], mask),
        _rwkv_solve(functools.partial(_rwkv_chunk_operands, head), [rstate_ref[gi] for gi in range(n_groups)],
                    mask, T // CHUNK))
    for gi in range(n_groups):
        rstate_ref[gi] = hcur[gi]
        tstate_ref[gi] = scur[gi]
    y = _head_norm(y, mask, RWKV_LN_EPS) * lng_ref[...] + lnb_ref[...]
    yr_ref[0] = ((y + head["bonus"]) * head["g"]).astype(BF16)
    gate = zt_ref[:, 3 * RET_WIDTH:4 * RET_WIDTH]
    yt_ref[0] = (gate * _sigmoid(gate) * (_head_norm(yt, mask, RET_GN_EPS) * gng_ref[...])).astype(BF16)


def _mixer(x, g, sc, sh, w_in, mu, w0, w2, a0, a2, g2, k_k, k_a, r_k, ln_g, ln_b,
           cos2, sin2, dmw, qdec, kdec, gcw, gn_g, mask):
    B, S, D = x.shape
    T = MIXER_TOKENS
    W = RWKV_WIDTH
    row = lambda b, t: (b, t, 0)
    per_b = lambda b, t: (b, 0, 0)
    vec = lambda a: a.reshape(1, -1)
    full = lambda a: pl.BlockSpec(a.shape, lambda b, t: (0, 0))
    tab = lambda a: pl.BlockSpec((T, a.shape[1]), lambda b, t: (t, 0))
    params = [vec(mu), vec(w0), w2.astype(BF16), vec(a0), a2.astype(BF16), g2.astype(BF16),
              vec(k_k), vec(k_a), vec(r_k), vec(ln_g), vec(ln_b)]
    tables = [dmw, qdec, kdec, gcw, vec(gn_g), mask]
    return pl.pallas_call(
        _mixer_kernel,
        grid=(B, S // T),
        in_specs=[pl.BlockSpec((1, T, D), row), full(vec(g)), pl.BlockSpec((1, 1, D), per_b),
                  pl.BlockSpec((1, 1, D), per_b), full(w_in)]
                 + [full(a) for a in params] + [tab(cos2), tab(sin2)] + [full(a) for a in tables],
        out_specs=[pl.BlockSpec((1, T, W), row), pl.BlockSpec((1, T, RET_WIDTH), row)],
        out_shape=[jax.ShapeDtypeStruct((B, S, W), BF16), jax.ShapeDtypeStruct((B, S, RET_WIDTH), BF16)],
        scratch_shapes=[pltpu.VMEM((T, RWKV_COLS), F32),
                        pltpu.VMEM((T, RET_COLS), F32),
                        pltpu.VMEM((SUBLANES, RWKV_COLS), F32),
                        pltpu.VMEM((W // GROUP, HEAD_DIM, GROUP), F32),
                        pltpu.VMEM((RET_WIDTH // GROUP, HEAD_DIM, GROUP), F32)],
        compiler_params=pltpu.CompilerParams(
            dimension_semantics=("parallel", "arbitrary"), vmem_limit_bytes=VMEM_LIMIT_BYTES),
        name="mixer",
    )(x, vec(g), sc, sh, w_in, *params, cos2, sin2, *tables)


def _ffn_kernel(x_ref, yr_ref, yt_ref, gta_ref, shf_ref, scf_ref, gtf_ref, wout_ref, gnf_ref,
                wup_ref, cw_ref, cb_ref, wdn_ref, gfin_ref, o_ref, carry_ref, ubuf_ref, abuf_ref):
    tm = x_ref.shape[1]

    @pl.when(pl.program_id(1) == 0)
    def _():
        carry_ref[...] = jnp.zeros_like(carry_ref)

    ymix = (jnp.dot(yr_ref[0], wout_ref[0:RWKV_WIDTH, :], preferred_element_type=F32)
            + jnp.dot(yt_ref[0], wout_ref[RWKV_WIDTH:, :], preferred_element_type=F32))
    x1 = x_ref[0] + gta_ref[0] * ymix
    q_rows = tm // SUBLANES

    def strided(t):
        return jnp.swapaxes(t.reshape(SUBLANES, q_rows, t.shape[1]), 0, 1).reshape(tm, t.shape[1])

    def unstrided(t):
        return jnp.swapaxes(t.reshape(q_rows, SUBLANES, t.shape[1]), 0, 1).reshape(tm, t.shape[1])

    h = strided(_rms(x1) * gnf_ref[...] * (1.0 + scf_ref[0]) + shf_ref[0]).astype(BF16)
    row8 = lax.broadcasted_iota(jnp.int32, (SUBLANES, 1), 0)

    def conv(u, col):
        tf = u.shape[1]
        prev = carry_ref[:, col:col + tf]
        last, before = u[tm - SUBLANES:tm], u[tm - 2 * SUBLANES:tm - SUBLANES]
        d1 = jnp.where(row8 == 0, prev[1:2, :], pltpu.roll(last, 1, 0))
        d2 = jnp.where(row8 == 0, prev[0:1, :], pltpu.roll(before, 1, 0))
        u1 = jnp.concatenate([d1, u[0:tm - SUBLANES]], axis=0)
        u2 = jnp.concatenate([d2, d1, u[0:tm - 2 * SUBLANES]], axis=0)
        carry_ref[0:1, col:col + tf] = before[SUBLANES - 1:SUBLANES, :]
        carry_ref[1:2, col:col + tf] = last[SUBLANES - 1:SUBLANES, :]
        cw = cw_ref[:, col:col + tf]
        return cb_ref[:, col:col + tf] + u2 * cw[0:1, :] + u1 * cw[1:2, :] + u * cw[2:3, :]

    tf = FFN_COLS
    tiles = list(range(0, D_FF, tf))

    def up(n):
        j, slot = tiles[n], n % 2
        ubuf_ref[slot, :, 0:tf] = jnp.dot(h, wup_ref[:, j:j + tf], preferred_element_type=F32)
        ubuf_ref[slot, :, tf:2 * tf] = jnp.dot(h, wup_ref[:, D_FF + j:D_FF + j + tf], preferred_element_type=F32)

    def glu(n):
        j, slot = tiles[n], n % 2
        val = conv(ubuf_ref[slot, :, 0:tf], j)
        gate = conv(ubuf_ref[slot, :, tf:2 * tf], D_FF + j)
        abuf_ref[:, j:j + tf] = (gate * _sigmoid(gate) * val).astype(BF16)

    up(0)
    for n in range(1, len(tiles)):
        up(n)
        glu(n - 1)
    glu(len(tiles) - 1)
    y = unstrided(jnp.dot(abuf_ref[...], wdn_ref[...], preferred_element_type=F32))
    x2 = x1 + gtf_ref[0] * y
    o_ref[0] = _rms(x2) * gfin_ref[...]


def _ffn(x, y_rwkv, y_ret, gt_a, sh_f, sc_f, gt_f, w_out, g_ffn, w_up, conv_w, conv_b, w_down, g_fin):
    B, S, D = x.shape
    tm = FFN_ROWS
    row = lambda b, t: (b, t, 0)
    per_b = lambda b, t: (b, 0, 0)
    const = lambda b, t: (0, 0)
    once = lambda a: pl.BlockSpec(a.shape, const, pipeline_mode=pl.Buffered(1))
    vecs = [g_ffn.reshape(1, D)]
    return pl.pallas_call(
        _ffn_kernel,
        grid=(B, S // tm),
        in_specs=[pl.BlockSpec((1, tm, D), row),
                  pl.BlockSpec((1, tm, RWKV_WIDTH), row),
                  pl.BlockSpec((1, tm, RET_WIDTH), row),
                  pl.BlockSpec((1, 1, D), per_b), pl.BlockSpec((1, 1, D), per_b),
                  pl.BlockSpec((1, 1, D), per_b), pl.BlockSpec((1, 1, D), per_b),
                  once(w_out), once(vecs[0]), once(w_up), once(conv_w), pl.BlockSpec((1, 2 * D_FF), const),
                  once(w_down), pl.BlockSpec((1, D), const)],
        out_specs=pl.BlockSpec((1, tm, D), row),
        out_shape=jax.ShapeDtypeStruct((B, S, D), F32),
        scratch_shapes=[pltpu.VMEM((SUBLANES, 2 * D_FF), F32),
                        pltpu.VMEM((2, tm, 2 * FFN_COLS), F32),
                        pltpu.VMEM((tm, D_FF), BF16)],
        compiler_params=pltpu.CompilerParams(
            dimension_semantics=("parallel", "arbitrary"), vmem_limit_bytes=VMEM_LIMIT_BYTES),
        name="ffn",
    )(x, y_rwkv, y_ret, gt_a, sh_f, sc_f, gt_f, w_out, vecs[0], w_up, conv_w,
      conv_b.reshape(1, 2 * D_FF), w_down, g_fin.reshape(1, D))


def _tables(S):
    pos = np.arange(S, dtype=np.float64)
    inv_freq = ROPE_BASE ** (-np.arange(0, HEAD_DIM, 2, dtype=np.float64) / HEAD_DIM)
    ang = pos[:, None] * inv_freq[None, :]
    cos, sin = np.cos(ang), np.sin(ang)
    cos2 = np.concatenate([cos, cos, cos, cos], axis=1)
    sin2 = np.concatenate([-sin, sin, -sin, sin], axis=1)
    log_gamma = np.log1p(-(2.0 ** (-5.0 - np.arange(N_HEADS, dtype=np.float64))))
    idx = np.arange(CHUNK, dtype=np.float64)
    dmat = np.exp(log_gamma[:, None, None] * np.abs(idx[:, None] - idx[None, :]))
    q_dec = np.exp(log_gamma[:, None] * (idx + 1.0))
    k_dec = np.exp(log_gamma[:, None] * (CHUNK - 1.0 - idx))
    widen = lambda t: np.repeat(t.T, HEAD_DIM, axis=1)
    dmw = dmat.transpose(1, 0, 2).reshape(CHUNK, N_HEADS * CHUNK)
    gcw = np.repeat(np.exp(log_gamma * CHUNK), HEAD_DIM)[None, :]
    head = np.arange(GROUP) // HEAD_DIM
    mask = head[:, None] == head[None, :]
    f32 = lambda t: jnp.asarray(t, F32)
    return f32(cos2), f32(sin2), f32(dmw), f32(widen(q_dec)), f32(widen(k_dec)), f32(gcw), jnp.asarray(mask, BF16)


def kernel(x, c, w_ada, b_ada, attn_norm_g, w_in, rwkv_mu, rwkv_w0, rwkv_w2, rwkv_a0, rwkv_a2, rwkv_g2,
           rwkv_k_k, rwkv_k_a, rwkv_r_k, rwkv_ln_g, rwkv_ln_b, ret_gn_g, w_out, ffn_norm_g, ffn_w_up,
           ffn_conv_w, ffn_conv_b, ffn_w_down, final_norm_g):
    B, S, D = x.shape
    assert D == D_MODEL and S % MIXER_TOKENS == 0 and S % FFN_ROWS == 0 and w_ada.shape[0] == 1
    cos2, sin2, dmw, qdec, kdec, gcw, mask = _tables(S)
    mod = _ada(c, w_ada[0], b_ada[0])
    sh_a, sc_a, gt_a, sh_f, sc_f, gt_f = [m.reshape(B, 1, D) for m in jnp.split(mod, N_MOD, axis=-1)]
    y_rwkv, y_ret = _mixer(x, attn_norm_g[0], sc_a, sh_a, w_in[0].astype(BF16),
                           rwkv_mu[0], rwkv_w0[0], rwkv_w2[0], rwkv_a0[0], rwkv_a2[0], rwkv_g2[0],
                           rwkv_k_k[0], rwkv_k_a[0], rwkv_r_k[0], rwkv_ln_g[0], rwkv_ln_b[0],
                           cos2, sin2, dmw, qdec, kdec, gcw, ret_gn_g[0], mask)
    return _ffn(x, y_rwkv, y_ret, gt_a, sh_f, sc_f, gt_f, w_out[0].astype(BF16), ffn_norm_g[0],
                ffn_w_up[0].astype(BF16), ffn_conv_w[0], ffn_conv_b[0], ffn_w_down[0].astype(BF16),
                final_norm_g)
```

```python
import functools
import math

import jax
import jax.numpy as jnp
import numpy as np
from jax import lax
from jax.experimental import pallas as pl
from jax.experimental.pallas import tpu as pltpu

F32 = jnp.float32
BF16 = jnp.bfloat16

D_MODEL = 1024
CHUNK = 64
HEAD_DIM = 64
RWKV_WIDTH = 512
RET_WIDTH = 512
N_HEADS = 8
DECAY_LORA = 64
AAA_LORA = 64
GATE_LORA = 128
RWKV_COLS = 3 * RWKV_WIDTH + DECAY_LORA + AAA_LORA + GATE_LORA
RET_COLS = 4 * RET_WIDTH
D_FF = 2816
ROPE_BASE = 10000.0
NORM_EPS = 1e-6
RWKV_LN_EPS = 64e-5
RET_GN_EPS = 1e-6
W_DECAY_SCALE = math.exp(-0.5)
N_MOD = 6
MXU_DIM = 256
SUBLANES = 8
GROUP = MXU_DIM
HEADS_PER_GROUP = GROUP // HEAD_DIM
KK_NORM_FLOOR = 1e-12

VMEM_LIMIT_BYTES = 56 * 1024 * 1024
MIXER_TOKENS = 512
FFN_ROWS = 512
FFN_COLS = MXU_DIM
INPROJ_COLS = MXU_DIM
ADA_COLS = 1536

_NT = (((1,), (1,)), ((), ()))


def _bdot(a, b):
    return jnp.dot(a.astype(BF16), b.astype(BF16), preferred_element_type=F32)


def _split_parts(x, parts=3):
    out, rem = [], x
    for i in range(parts):
        p = rem.astype(BF16)
        out.append(p)
        if i + 1 < parts:
            rem = rem - p.astype(F32)
    return out


def _head_sums(x, mask, parts=2):
    outs = []
    for g in range(x.shape[1] // GROUP):
        acc = None
        for p in _split_parts(x[:, g * GROUP:(g + 1) * GROUP], parts=parts):
            d = jnp.dot(p, mask, preferred_element_type=F32)
            acc = d if acc is None else acc + d
        outs.append(acc)
    return jnp.concatenate(outs, axis=1)


def _exact_dot_left(m_bf16, x):
    acc = None
    for p in _split_parts(x):
        d = jnp.dot(m_bf16, p, preferred_element_type=F32)
        acc = d if acc is None else acc + d
    return acc


def _sigmoid(x):
    return 1.0 / (1.0 + jnp.exp(-x))


def _rms(x):
    return x * lax.rsqrt(jnp.mean(x * x, axis=-1, keepdims=True) + NORM_EPS)


def _ada_kernel(c_ref, w_ref, b_ref, o_ref):
    cv = c_ref[...]
    s = cv * _sigmoid(cv)
    o_ref[...] = jnp.dot(s, w_ref[...], preferred_element_type=F32,
                         precision=lax.Precision.HIGHEST) + b_ref[...]


def _ada(c, w, b):
    B = c.shape[0]
    n = w.shape[1]
    return pl.pallas_call(
        _ada_kernel,
        grid=(n // ADA_COLS,),
        in_specs=[pl.BlockSpec((B, D_MODEL), lambda j: (0, 0)),
                  pl.BlockSpec((D_MODEL, ADA_COLS), lambda j: (0, j)),
                  pl.BlockSpec((1, ADA_COLS), lambda j: (0, j))],
        out_specs=pl.BlockSpec((B, ADA_COLS), lambda j: (0, j)),
        out_shape=jax.ShapeDtypeStruct((B, n), F32),
        compiler_params=pltpu.CompilerParams(vmem_limit_bytes=VMEM_LIMIT_BYTES),
        name="ada",
    )(c, w, b.reshape(1, n))


def _head_norm(y, mask, eps):
    mean = _head_sums(y, mask) * (1.0 / HEAD_DIM)
    d = y - mean
    var = _head_sums(d * d, mask, parts=1) * (1.0 / HEAD_DIM)
    return d * lax.rsqrt(var + eps)


def _shift_rows(x, first):
    rolled = pltpu.roll(x, 1, 0)
    head = jnp.where(lax.broadcasted_iota(jnp.int32, (SUBLANES, 1), 0) == 0, first, rolled[0:SUBLANES])
    return jnp.concatenate([head, rolled[SUBLANES:]], axis=0)


def _bd(x, mask):
    xb = x.astype(BF16)
    half = GROUP // 2
    zeros = jnp.zeros((HEAD_DIM, half), BF16)
    blocks = []
    for h in range(HEADS_PER_GROUP):
        rows = slice(h * HEAD_DIM, (h + 1) * HEAD_DIM)
        if h < HEADS_PER_GROUP // 2:
            blocks.append(jnp.concatenate([xb[:, 0:half] * mask[rows, 0:half], zeros], axis=1))
        else:
            blocks.append(jnp.concatenate([zeros, xb[:, half:GROUP] * mask[rows, half:GROUP]], axis=1))
    return jnp.concatenate(blocks, axis=0)


def _block_transpose(x):
    r = jnp.concatenate([x] * HEADS_PER_GROUP, axis=0).T
    blk = lax.broadcasted_iota(jnp.int32, (1, GROUP), 1) // HEAD_DIM
    out = r[0:HEAD_DIM]
    for h in range(1, HEADS_PER_GROUP):
        out = jnp.where(blk == h, r[h * HEAD_DIM:(h + 1) * HEAD_DIM], out)
    return out


def _wdot(lhs, rhs_bd):
    return jnp.dot(lhs.astype(BF16), rhs_bd, preferred_element_type=F32)


def _wdot_nt(lhs, rhs_bd):
    return lax.dot_general(lhs.astype(BF16), rhs_bd, _NT, preferred_element_type=F32)


def _interleave(*gens):
    results = [None] * len(gens)
    live = list(range(len(gens)))
    while live:
        for i in list(live):
            try:
                next(gens[i])
            except StopIteration as stop:
                results[i] = stop.value
                live.remove(i)
    return results


def _rwkv_prep(z, zfirst, prm, mask):
    W = RWKV_WIDTH
    zprev = _shift_rows(z, zfirst)
    zm = z + prm["mu"] * (zprev - z)
    r = zm[:, 0:W]
    k = zm[:, W:2 * W]
    v = zm[:, 2 * W:3 * W]
    wd = zm[:, 3 * W:3 * W + DECAY_LORA]
    ad = zm[:, 3 * W + DECAY_LORA:3 * W + DECAY_LORA + AAA_LORA]
    gd = zm[:, 3 * W + DECAY_LORA + AAA_LORA:RWKV_COLS]

    lw = -W_DECAY_SCALE * _sigmoid(prm["w0"] + _bdot(jnp.tanh(wd), prm["w2"]))
    a = _sigmoid(prm["a0"] + _bdot(ad, prm["a2"]))
    g = _bdot(_sigmoid(gd), prm["g2"])
    kkr = k * prm["k_k"]
    yield
    kkn = kkr * lax.rsqrt(jnp.maximum(_head_sums(kkr * kkr, mask, parts=1), KK_NORM_FLOOR ** 2))
    km = k * (1.0 + (a - 1.0) * prm["k_a"])
    yield
    bonus = _head_sums(r * km * prm["r_k"], mask) * v
    return dict(r=r, v=v, lw=lw, kkn=kkn, km=km, kb=kkn * a, bonus=bonus, g=g)


def _rwkv_chunk_operands(head, c):
    C = CHUNK
    rows = slice(c * C, (c + 1) * C)
    ri = lax.broadcasted_iota(jnp.int32, (C, C), 0)
    ci = lax.broadcasted_iota(jnp.int32, (C, C), 1)
    lw_c = head["lw"][rows]
    lp = _exact_dot_left((ri >= ci).astype(BF16), lw_c)
    em = jnp.exp(-lp)
    lpc = lp[C - 1:C, :]
    ee = jnp.exp(lpc - lp)
    kb, km = head["kb"][rows], head["km"][rows]
    fields = dict(rt=head["r"][rows] * jnp.exp(lp), at=-head["kkn"][rows] * jnp.exp(lp - lw_c),
                  bt=kb * em, kt=km * em, b_d=kb * ee, k_d=km * ee, v=head["v"][rows])
    return fields, lpc


def _rwkv_solve(get, hcur, mask, n_chunks):
    C = CHUNK
    n_groups = RWKV_WIDTH // GROUP
    rw = lax.broadcasted_iota(jnp.int32, (C, GROUP), 0)
    cw = lax.broadcasted_iota(jnp.int32, (C, GROUP), 1) % HEAD_DIM
    strict = rw > cw
    incl = rw >= cw
    eye = (rw == cw).astype(F32)

    def scores(u):
        lhs = jnp.concatenate([u["at"], u["rt"]], axis=0)
        sb = _wdot_nt(lhs, _bd(u["bt"], mask))
        sk = _wdot_nt(lhs, _bd(u["kt"], mask))
        u["p"] = jnp.where(strict, sb[0:C], 0.0)
        u["m_rb"] = jnp.where(incl, sb[C:2 * C], 0.0)
        u["l_ak"] = jnp.where(strict, sk[0:C], 0.0)
        u["m_rk"] = jnp.where(incl, sk[C:2 * C], 0.0)
        u["tinv"] = eye + u["p"]
        u["bt_b"] = _block_transpose(u["b_d"])
        u["bt_k"] = _block_transpose(u["k_d"])
        u["pcw"] = _block_transpose(u["pc"])

    def square(u):
        u["p"] = _wdot(u["p"], _bd(u["p"], mask))

    def double(u):
        both = _wdot(jnp.concatenate([u["p"], u["tinv"]], axis=0), _bd(u["p"], mask))
        u["p"] = both[0:C]
        u["tinv"] = u["tinv"] + both[C:2 * C]

    def double_last(u):
        u["tinv"] = u["tinv"] + _wdot(u["tinv"], _bd(u["p"], mask))

    def with_v(u):
        rv = _wdot(jnp.concatenate([u["l_ak"], u["m_rk"], u["bt_k"]], axis=0), _bd(u["v"], mask))
        u["x"], u["y0"], u["psi"] = rv[0:C], rv[C:2 * C], rv[2 * C:3 * C]

    def times_t(u):
        u["gt"] = _wdot(jnp.concatenate([u["bt_b"], u["m_rb"]], axis=0), _bd(u["tinv"], mask))

    def fold(u):
        rw_ = _wdot(u["gt"], _bd(u["at"], mask))
        ru = _wdot(u["gt"], _bd(u["x"], mask))
        u["po"] = jnp.concatenate([rw_[0:C], u["rt"] + rw_[C:2 * C]], axis=0)
        u["psi"] = u["psi"] + ru[0:C]
        u["y0"] = u["y0"] + ru[C:2 * C]

    hcur = list(hcur)
    ys = [[None] * n_groups for _ in range(n_chunks)]

    def carry(u):
        c, gi = u["c"], u["g"]
        ph = _wdot(u["po"], _bd(hcur[gi], mask))
        ys[c][gi] = ph[C:2 * C] + u["y0"]
        hcur[gi] = u["pcw"] * hcur[gi] + ph[0:C] + u["psi"]

    stages = [scores, square, double, double, double, double, double_last, with_v, times_t, fold, carry]

    units = {}
    for t in range(n_chunks + len(stages)):
        for c in reversed(range(n_chunks)):
            st = t - c - 1
            if st == -1:
                fields, lpc = get(c)
                pc = jnp.broadcast_to(jnp.exp(lpc), (C, RWKV_WIDTH))
                units[c] = []
                for gi in range(n_groups):
                    ln = slice(gi * GROUP, (gi + 1) * GROUP)
                    u = {name: val[:, ln] for name, val in fields.items()}
                    u.update(c=c, g=gi, pc=pc[:, ln])
                    units[c].append(u)
            elif 0 <= st < len(stages):
                for u in units[c]:
                    stages[st](u)
        yield
    y = jnp.concatenate([jnp.concatenate(ys[c], axis=1) for c in range(n_chunks)], axis=0)
    return y, hcur


def _ret_mix(z_ref, cos_ref, sin_ref, dm_ref, qdec_ref, kdec_ref, gc_ref, scur, mask):
    C = CHUNK
    W = RET_WIDTH
    T = z_ref.shape[0]
    n_chunks = T // C
    n_groups = W // GROUP
    reps = W // cos_ref.shape[1]
    cosf = jnp.concatenate([cos_ref[...]] * reps, axis=1)
    sinf = jnp.concatenate([sin_ref[...]] * reps, axis=1)
    lane = lax.broadcasted_iota(jnp.int32, (1, W), 1)
    first_half = (lane % HEAD_DIM) < (HEAD_DIM // 2)

    def rot(t):
        swapped = jnp.where(first_half, pltpu.roll(t, W - HEAD_DIM // 2, 1), pltpu.roll(t, HEAD_DIM // 2, 1))
        return t * cosf + swapped * sinf

    q = rot(z_ref[:, 0:W]) * (HEAD_DIM ** -0.5)
    yield
    k = rot(z_ref[:, W:2 * W])
    v = z_ref[:, 2 * W:3 * W]
    yield

    units = []
    for c in range(n_chunks):
        rows = slice(c * C, (c + 1) * C)
        qd = q[rows] * qdec_ref[...]
        kd = k[rows] * kdec_ref[...]
        for gi in range(n_groups):
            ln = slice(gi * GROUP, (gi + 1) * GROUP)
            units.append(dict(c=c, g=gi, q=q[rows, ln], k=k[rows, ln], v=v[rows, ln], qd=qd[:, ln], kd=kd[:, ln]))
    for u in units:
        u["sc"] = _wdot_nt(u["q"], _bd(u["k"], mask)) * dm_ref[:, u["g"] * GROUP:(u["g"] + 1) * GROUP]
        u["kdt"] = _block_transpose(u["kd"])
    yield
    for u in units:
        both = _wdot(jnp.concatenate([u["sc"], u["kdt"]], axis=0), _bd(u["v"], mask))
        u["y"], u["kv"] = both[0:C], both[C:2 * C]
    yield

    scur = list(scur)
    ys = [[None] * n_groups for _ in range(n_chunks)]
    for u in units:
        c, gi = u["c"], u["g"]
        ys[c][gi] = u["y"] + _wdot(u["qd"], _bd(scur[gi], mask))
        scur[gi] = scur[gi] * gc_ref[:, gi * GROUP:(gi + 1) * GROUP] + u["kv"]
    y = jnp.concatenate([jnp.concatenate(ys[c], axis=1) for c in range(n_chunks)], axis=0)
    return y, scur


def _mixer_kernel(x_ref, g_ref, sc_ref, sh_ref, win_ref,
                  mu_ref, w0_ref, w2_ref, a0_ref, a2_ref, g2_ref, kk_ref, ka_ref, rk_ref, lng_ref, lnb_ref,
                  cos_ref, sin_ref, dm_ref, qdec_ref, kdec_ref, gc_ref, gng_ref, mask_ref,
                  yr_ref, yt_ref, zr_ref, zt_ref, zlast_ref, rstate_ref, tstate_ref):
    T = x_ref.shape[1]
    n_groups = RWKV_WIDTH // GROUP

    @pl.when(pl.program_id(1) == 0)
    def _():
        zlast_ref[...] = jnp.zeros_like(zlast_ref)
        rstate_ref[...] = jnp.zeros_like(rstate_ref)
        tstate_ref[...] = jnp.zeros_like(tstate_ref)

    h = (_rms(x_ref[0]) * g_ref[...] * (1.0 + sc_ref[0]) + sh_ref[0]).astype(BF16)
    step = INPROJ_COLS
    for j in range(0, RWKV_COLS, step):
        zr_ref[:, j:j + step] = jnp.dot(h, win_ref[:, j:j + step], preferred_element_type=F32)

    def ret_proj():
        for j in range(0, RET_COLS, step):
            zt_ref[:, j:j + step] = jnp.dot(h, win_ref[:, RWKV_COLS + j:RWKV_COLS + j + step],
                                            preferred_element_type=F32)
            yield

    mask = mask_ref[...]
    prm = dict(mu=mu_ref[...], w0=w0_ref[...], w2=w2_ref[...], a0=a0_ref[...], a2=a2_ref[...],
               g2=g2_ref[...], k_k=kk_ref[...], k_a=ka_ref[...], r_k=rk_ref[...])
    _, head = _interleave(ret_proj(), _rwkv_prep(zr_ref[...], zlast_ref[0:1, :], prm, mask))
    zlast_ref[0:1, :] = zr_ref[T - 1:T, :]

    (yt, scur), (y, hcur) = _interleave(
        _ret_mix(zt_ref, cos_ref, sin_ref, dm_ref, qdec_ref, kdec_ref, gc_ref,
                 [tstate_ref[gi] for gi in range(n_groups)], mask),
        _rwkv_solve(functools.partial(_rwkv_chunk_operands, head), [rstate_ref[gi] for gi in range(n_groups)],
                    mask, T // CHUNK))
    for gi in range(n_groups):
        rstate_ref[gi] = hcur[gi]
        tstate_ref[gi] = scur[gi]
    y = _head_norm(y, mask, RWKV_LN_EPS) * lng_ref[...] + lnb_ref[...]
    yr_ref[0] = ((y + head["bonus"]) * head["g"]).astype(BF16)
    gate = zt_ref[:, 3 * RET_WIDTH:4 * RET_WIDTH]
    yt_ref[0] = (gate * _sigmoid(gate) * (_head_norm(yt, mask, RET_GN_EPS) * gng_ref[...])).astype(BF16)


def _mixer(x, g, sc, sh, w_in, mu, w0, w2, a0, a2, g2, k_k, k_a, r_k, ln_g, ln_b,
           cos2, sin2, dmw, qdec, kdec, gcw, gn_g, mask):
    B, S, D = x.shape
    T = MIXER_TOKENS
    W = RWKV_WIDTH
    row = lambda b, t: (b, t, 0)
    per_b = lambda b, t: (b, 0, 0)
    vec = lambda a: a.reshape(1, -1)
    full = lambda a: pl.BlockSpec(a.shape, lambda b, t: (0, 0))
    tab = lambda a: pl.BlockSpec((T, a.shape[1]), lambda b, t: (t, 0))
    params = [vec(mu), vec(w0), w2.astype(BF16), vec(a0), a2.astype(BF16), g2.astype(BF16),
              vec(k_k), vec(k_a), vec(r_k), vec(ln_g), vec(ln_b)]
    tables = [dmw, qdec, kdec, gcw, vec(gn_g), mask]
    return pl.pallas_call(
        _mixer_kernel,
        grid=(B, S // T),
        in_specs=[pl.BlockSpec((1, T, D), row), full(vec(g)), pl.BlockSpec((1, 1, D), per_b),
                  pl.BlockSpec((1, 1, D), per_b), full(w_in)]
                 + [full(a) for a in params] + [tab(cos2), tab(sin2)] + [full(a) for a in tables],
        out_specs=[pl.BlockSpec((1, T, W), row), pl.BlockSpec((1, T, RET_WIDTH), row)],
        out_shape=[jax.ShapeDtypeStruct((B, S, W), BF16), jax.ShapeDtypeStruct((B, S, RET_WIDTH), BF16)],
        scratch_shapes=[pltpu.VMEM((T, RWKV_COLS), F32),
                        pltpu.VMEM((T, RET_COLS), F32),
                        pltpu.VMEM((SUBLANES, RWKV_COLS), F32),
                        pltpu.VMEM((W // GROUP, HEAD_DIM, GROUP), F32),
                        pltpu.VMEM((RET_WIDTH // GROUP, HEAD_DIM, GROUP), F32)],
        compiler_params=pltpu.CompilerParams(
            dimension_semantics=("parallel", "arbitrary"), vmem_limit_bytes=VMEM_LIMIT_BYTES),
        name="mixer",
    )(x, vec(g), sc, sh, w_in, *params, cos2, sin2, *tables)


def _ffn_kernel(x_ref, yr_ref, yt_ref, gta_ref, shf_ref, scf_ref, gtf_ref, wout_ref, gnf_ref,
                wup_ref, cw_ref, cb_ref, wdn_ref, gfin_ref, o_ref, carry_ref, ubuf_ref, abuf_ref):
    tm = x_ref.shape[1]

    @pl.when(pl.program_id(1) == 0)
    def _():
        carry_ref[...] = jnp.zeros_like(carry_ref)

    ymix = (jnp.dot(yr_ref[0], wout_ref[0:RWKV_WIDTH, :], preferred_element_type=F32)
            + jnp.dot(yt_ref[0], wout_ref[RWKV_WIDTH:, :], preferred_element_type=F32))
    x1 = x_ref[0] + gta_ref[0] * ymix
    q_rows = tm // SUBLANES

    def strided(t):
        return jnp.swapaxes(t.reshape(SUBLANES, q_rows, t.shape[1]), 0, 1).reshape(tm, t.shape[1])

    def unstrided(t):
        return jnp.swapaxes(t.reshape(q_rows, SUBLANES, t.shape[1]), 0, 1).reshape(tm, t.shape[1])

    h = strided(_rms(x1) * gnf_ref[...] * (1.0 + scf_ref[0]) + shf_ref[0]).astype(BF16)
    row8 = lax.broadcasted_iota(jnp.int32, (SUBLANES, 1), 0)

    def conv(u, col):
        tf = u.shape[1]
        prev = carry_ref[:, col:col + tf]
        last, before = u[tm - SUBLANES:tm], u[tm - 2 * SUBLANES:tm - SUBLANES]
        d1 = jnp.where(row8 == 0, prev[1:2, :], pltpu.roll(last, 1, 0))
        d2 = jnp.where(row8 == 0, prev[0:1, :], pltpu.roll(before, 1, 0))
        u1 = jnp.concatenate([d1, u[0:tm - SUBLANES]], axis=0)
        u2 = jnp.concatenate([d2, d1, u[0:tm - 2 * SUBLANES]], axis=0)
        carry_ref[0:1, col:col + tf] = before[SUBLANES - 1:SUBLANES, :]
        carry_ref[1:2, col:col + tf] = last[SUBLANES - 1:SUBLANES, :]
        cw = cw_ref[:, col:col + tf]
        return cb_ref[:, col:col + tf] + u2 * cw[0:1, :] + u1 * cw[1:2, :] + u * cw[2:3, :]

    tf = FFN_COLS
    tiles = list(range(0, D_FF, tf))

    def up(n):
        j, slot = tiles[n], n % 2
        ubuf_ref[slot, :, 0:tf] = jnp.dot(h, wup_ref[:, j:j + tf], preferred_element_type=F32)
        ubuf_ref[slot, :, tf:2 * tf] = jnp.dot(h, wup_ref[:, D_FF + j:D_FF + j + tf], preferred_element_type=F32)

    def glu(n):
        j, slot = tiles[n], n % 2
        val = conv(ubuf_ref[slot, :, 0:tf], j)
        gate = conv(ubuf_ref[slot, :, tf:2 * tf], D_FF + j)
        abuf_ref[:, j:j + tf] = (gate * _sigmoid(gate) * val).astype(BF16)

    up(0)
    for n in range(1, len(tiles)):
        up(n)
        glu(n - 1)
    glu(len(tiles) - 1)
    y = unstrided(jnp.dot(abuf_ref[...], wdn_ref[...], preferred_element_type=F32))
    x2 = x1 + gtf_ref[0] * y
    o_ref[0] = _rms(x2) * gfin_ref[...]


def _ffn(x, y_rwkv, y_ret, gt_a, sh_f, sc_f, gt_f, w_out, g_ffn, w_up, conv_w, conv_b, w_down, g_fin):
    B, S, D = x.shape
    tm = FFN_ROWS
    row = lambda b, t: (b, t, 0)
    per_b = lambda b, t: (b, 0, 0)
    const = lambda b, t: (0, 0)
    once = lambda a: pl.BlockSpec(a.shape, const, pipeline_mode=pl.Buffered(1))
    vecs = [g_ffn.reshape(1, D)]
    return pl.pallas_call(
        _ffn_kernel,
        grid=(B, S // tm),
        in_specs=[pl.BlockSpec((1, tm, D), row),
                  pl.BlockSpec((1, tm, RWKV_WIDTH), row),
                  pl.BlockSpec((1, tm, RET_WIDTH), row),
                  pl.BlockSpec((1, 1, D), per_b), pl.BlockSpec((1, 1, D), per_b),
                  pl.BlockSpec((1, 1, D), per_b), pl.BlockSpec((1, 1, D), per_b),
                  once(w_out), once(vecs[0]), once(w_up), once(conv_w), pl.BlockSpec((1, 2 * D_FF), const),
                  once(w_down), pl.BlockSpec((1, D), const)],
        out_specs=pl.BlockSpec((1, tm, D), row),
        out_shape=jax.ShapeDtypeStruct((B, S, D), F32),
        scratch_shapes=[pltpu.VMEM((SUBLANES, 2 * D_FF), F32),
                        pltpu.VMEM((2, tm, 2 * FFN_COLS), F32),
                        pltpu.VMEM((tm, D_FF), BF16)],
        compiler_params=pltpu.CompilerParams(
            dimension_semantics=("parallel", "arbitrary"), vmem_limit_bytes=VMEM_LIMIT_BYTES),
        name="ffn",
    )(x, y_rwkv, y_ret, gt_a, sh_f, sc_f, gt_f, w_out, vecs[0], w_up, conv_w,
      conv_b.reshape(1, 2 * D_FF), w_down, g_fin.reshape(1, D))


def _tables(S):
    pos = np.arange(S, dtype=np.float64)
    inv_freq = ROPE_BASE ** (-np.arange(0, HEAD_DIM, 2, dtype=np.float64) / HEAD_DIM)
    ang = pos[:, None] * inv_freq[None, :]
    cos, sin = np.cos(ang), np.sin(ang)
    cos2 = np.concatenate([cos, cos, cos, cos], axis=1)
    sin2 = np.concatenate([-sin, sin, -sin, sin], axis=1)
    log_gamma = np.log1p(-(2.0 ** (-5.0 - np.arange(N_HEADS, dtype=np.float64))))
    idx = np.arange(CHUNK, dtype=np.float64)
    dmat = np.exp(log_gamma[:, None, None] * np.abs(idx[:, None] - idx[None, :]))
    q_dec = np.exp(log_gamma[:, None] * (idx + 1.0))
    k_dec = np.exp(log_gamma[:, None] * (CHUNK - 1.0 - idx))
    widen = lambda t: np.repeat(t.T, HEAD_DIM, axis=1)
    dmw = dmat.transpose(1, 0, 2).reshape(CHUNK, N_HEADS * CHUNK)
    gcw = np.repeat(np.exp(log_gamma * CHUNK), HEAD_DIM)[None, :]
    head = np.arange(GROUP) // HEAD_DIM
    mask = head[:, None] == head[None, :]
    f32 = lambda t: jnp.asarray(t, F32)
    return f32(cos2), f32(sin2), f32(dmw), f32(widen(q_dec)), f32(widen(k_dec)), f32(gcw), jnp.asarray(mask, BF16)


def kernel(x, c, w_ada, b_ada, attn_norm_g, w_in, rwkv_mu, rwkv_w0, rwkv_w2, rwkv_a0, rwkv_a2, rwkv_g2,
           rwkv_k_k, rwkv_k_a, rwkv_r_k, rwkv_ln_g, rwkv_ln_b, ret_gn_g, w_out, ffn_norm_g, ffn_w_up,
           ffn_conv_w, ffn_conv_b, ffn_w_down, final_norm_g):
    B, S, D = x.shape
    assert D == D_MODEL and S % MIXER_TOKENS == 0 and S % FFN_ROWS == 0 and w_ada.shape[0] == 1
    cos2, sin2, dmw, qdec, kdec, gcw, mask = _tables(S)
    mod = _ada(c, w_ada[0], b_ada[0])
    sh_a, sc_a, gt_a, sh_f, sc_f, gt_f = [m.reshape(B, 1, D) for m in jnp.split(mod, N_MOD, axis=-1)]
    y_rwkv, y_ret = _mixer(x, attn_norm_g[0], sc_a, sh_a, w_in[0].astype(BF16),
                           rwkv_mu[0], rwkv_w0[0], rwkv_w2[0], rwkv_a0[0], rwkv_a2[0], rwkv_g2[0],
                           rwkv_k_k[0], rwkv_k_a[0], rwkv_r_k[0], rwkv_ln_g[0], rwkv_ln_b[0],
                           cos2, sin2, dmw, qdec, kdec, gcw, ret_gn_g[0], mask)
    return _ffn(x, y_rwkv, y_ret, gt_a, sh_f, sc_f, gt_f, w_out[0].astype(BF16), ffn_norm_g[0],
                ffn_w_up[0].astype(BF16), ffn_conv_w[0], ffn_conv_b[0], ffn_w_down[0].astype(BF16),
                final_norm_g)
```

```python
import functools
import math

import jax
import jax.numpy as jnp
import numpy as np
from jax import lax
from jax.experimental import pallas as pl
from jax.experimental.pallas import tpu as pltpu

F32 = jnp.float32
BF16 = jnp.bfloat16

D_MODEL = 1024
CHUNK = 64
HEAD_DIM = 64
RWKV_WIDTH = 512
RET_WIDTH = 512
N_HEADS = 8
DECAY_LORA = 64
AAA_LORA = 64
GATE_LORA = 128
RWKV_COLS = 3 * RWKV_WIDTH + DECAY_LORA + AAA_LORA + GATE_LORA
RET_COLS = 4 * RET_WIDTH
D_FF = 2816
ROPE_BASE = 10000.0
NORM_EPS = 1e-6
RWKV_LN_EPS = 64e-5
RET_GN_EPS = 1e-6
W_DECAY_SCALE = math.exp(-0.5)
N_MOD = 6
MXU_DIM = 256
SUBLANES = 8
GROUP = MXU_DIM
HEADS_PER_GROUP = GROUP // HEAD_DIM
KK_NORM_FLOOR = 1e-12

VMEM_LIMIT_BYTES = 56 * 1024 * 1024
MIXER_TOKENS = 512
FFN_ROWS = 512
FFN_COLS = MXU_DIM
INPROJ_COLS = MXU_DIM
ADA_COLS = 1536
CAST_ROWS = 256
CAST_COLS = 512

_NT = (((1,), (1,)), ((), ()))


def _bdot(a, b):
    return jnp.dot(a.astype(BF16), b.astype(BF16), preferred_element_type=F32)


def _split_parts(x, parts=3):
    out, rem = [], x
    for i in range(parts):
        p = rem.astype(BF16)
        out.append(p)
        if i + 1 < parts:
            rem = rem - p.astype(F32)
    return out


def _head_sums(x, mask, parts=2):
    outs = []
    for g in range(x.shape[1] // GROUP):
        acc = None
        for p in _split_parts(x[:, g * GROUP:(g + 1) * GROUP], parts=parts):
            d = jnp.dot(p, mask, preferred_element_type=F32)
            acc = d if acc is None else acc + d
        outs.append(acc)
    return jnp.concatenate(outs, axis=1)


def _exact_dot_left(m_bf16, x):
    acc = None
    for p in _split_parts(x):
        d = jnp.dot(m_bf16, p, preferred_element_type=F32)
        acc = d if acc is None else acc + d
    return acc


def _cast_weight(w_hbm, dst_ref, stg_ref, sem_ref, block_cols):
    rows, cols = dst_ref.shape
    block_rows = stg_ref.shape[1]
    blocks = [(r, c) for r in range(0, rows, block_rows) for c in range(0, cols, block_cols)]

    def copy(k):
        r, c = blocks[k]
        return pltpu.make_async_copy(w_hbm.at[pl.ds(r, block_rows), pl.ds(c, block_cols)],
                                     stg_ref.at[k % 2, :, pl.ds(0, block_cols)], sem_ref.at[k % 2])

    copy(0).start()
    for k, (r, c) in enumerate(blocks):
        if k + 1 < len(blocks):
            copy(k + 1).start()
        copy(k).wait()
        dst_ref[r:r + block_rows, c:c + block_cols] = stg_ref[k % 2, :, 0:block_cols].astype(BF16)


def _sigmoid(x):
    return 1.0 / (1.0 + jnp.exp(-x))


def _rms(x):
    return x * lax.rsqrt(jnp.mean(x * x, axis=-1, keepdims=True) + NORM_EPS)


def _ada_kernel(c_ref, w_ref, b_ref, o_ref):
    cv = c_ref[...]
    s = cv * _sigmoid(cv)
    o_ref[...] = jnp.dot(s, w_ref[...], preferred_element_type=F32,
                         precision=lax.Precision.HIGHEST) + b_ref[...]


def _ada(c, w, b):
    B = c.shape[0]
    n = w.shape[1]
    return pl.pallas_call(
        _ada_kernel,
        grid=(n // ADA_COLS,),
        in_specs=[pl.BlockSpec((B, D_MODEL), lambda j: (0, 0)),
                  pl.BlockSpec((D_MODEL, ADA_COLS), lambda j: (0, j)),
                  pl.BlockSpec((1, ADA_COLS), lambda j: (0, j))],
        out_specs=pl.BlockSpec((B, ADA_COLS), lambda j: (0, j)),
        out_shape=jax.ShapeDtypeStruct((B, n), F32),
        compiler_params=pltpu.CompilerParams(vmem_limit_bytes=VMEM_LIMIT_BYTES),
        name="ada",
    )(c, w, b.reshape(1, n))


def _head_norm(y, mask, eps):
    mean = _head_sums(y, mask) * (1.0 / HEAD_DIM)
    d = y - mean
    var = _head_sums(d * d, mask, parts=1) * (1.0 / HEAD_DIM)
    return d * lax.rsqrt(var + eps)


def _shift_rows(x, first):
    rolled = pltpu.roll(x, 1, 0)
    head = jnp.where(lax.broadcasted_iota(jnp.int32, (SUBLANES, 1), 0) == 0, first, rolled[0:SUBLANES])
    return jnp.concatenate([head, rolled[SUBLANES:]], axis=0)


def _bd(x, mask):
    xb = x.astype(BF16)
    half = GROUP // 2
    zeros = jnp.zeros((HEAD_DIM, half), BF16)
    blocks = []
    for h in range(HEADS_PER_GROUP):
        rows = slice(h * HEAD_DIM, (h + 1) * HEAD_DIM)
        if h < HEADS_PER_GROUP // 2:
            blocks.append(jnp.concatenate([xb[:, 0:half] * mask[rows, 0:half], zeros], axis=1))
        else:
            blocks.append(jnp.concatenate([zeros, xb[:, half:GROUP] * mask[rows, half:GROUP]], axis=1))
    return jnp.concatenate(blocks, axis=0)


def _block_transpose(x):
    r = jnp.concatenate([x] * HEADS_PER_GROUP, axis=0).T
    blk = lax.broadcasted_iota(jnp.int32, (1, GROUP), 1) // HEAD_DIM
    out = r[0:HEAD_DIM]
    for h in range(1, HEADS_PER_GROUP):
        out = jnp.where(blk == h, r[h * HEAD_DIM:(h + 1) * HEAD_DIM], out)
    return out


def _wdot(lhs, rhs_bd):
    return jnp.dot(lhs.astype(BF16), rhs_bd, preferred_element_type=F32)


def _wdot_nt(lhs, rhs_bd):
    return lax.dot_general(lhs.astype(BF16), rhs_bd, _NT, preferred_element_type=F32)


def _interleave(*gens):
    results = [None] * len(gens)
    live = list(range(len(gens)))
    while live:
        for i in list(live):
            try:
                next(gens[i])
            except StopIteration as stop:
                results[i] = stop.value
                live.remove(i)
    return results


def _rwkv_prep(z, zfirst, prm, mask):
    W = RWKV_WIDTH
    zprev = _shift_rows(z, zfirst)
    zm = z + prm["mu"] * (zprev - z)
    r = zm[:, 0:W]
    k = zm[:, W:2 * W]
    v = zm[:, 2 * W:3 * W]
    wd = zm[:, 3 * W:3 * W + DECAY_LORA]
    ad = zm[:, 3 * W + DECAY_LORA:3 * W + DECAY_LORA + AAA_LORA]
    gd = zm[:, 3 * W + DECAY_LORA + AAA_LORA:RWKV_COLS]

    lw = -W_DECAY_SCALE * _sigmoid(prm["w0"] + _bdot(jnp.tanh(wd), prm["w2"]))
    a = _sigmoid(prm["a0"] + _bdot(ad, prm["a2"]))
    g = _bdot(_sigmoid(gd), prm["g2"])
    kkr = k * prm["k_k"]
    yield
    kkn = kkr * lax.rsqrt(jnp.maximum(_head_sums(kkr * kkr, mask, parts=1), KK_NORM_FLOOR ** 2))
    km = k * (1.0 + (a - 1.0) * prm["k_a"])
    yield
    bonus = _head_sums(r * km * prm["r_k"], mask) * v
    return dict(r=r, v=v, lw=lw, kkn=kkn, km=km, kb=kkn * a, bonus=bonus, g=g)


def _rwkv_chunk_operands(head, c):
    C = CHUNK
    rows = slice(c * C, (c + 1) * C)
    ri = lax.broadcasted_iota(jnp.int32, (C, C), 0)
    ci = lax.broadcasted_iota(jnp.int32, (C, C), 1)
    lw_c = head["lw"][rows]
    lp = _exact_dot_left((ri >= ci).astype(BF16), lw_c)
    em = jnp.exp(-lp)
    lpc = lp[C - 1:C, :]
    ee = jnp.exp(lpc - lp)
    kb, km = head["kb"][rows], head["km"][rows]
    fields = dict(rt=head["r"][rows] * jnp.exp(lp), at=-head["kkn"][rows] * jnp.exp(lp - lw_c),
                  bt=kb * em, kt=km * em, b_d=kb * ee, k_d=km * ee, v=head["v"][rows])
    return fields, lpc


def _rwkv_solve(get, hcur, mask, n_chunks):
    C = CHUNK
    n_groups = RWKV_WIDTH // GROUP
    rw = lax.broadcasted_iota(jnp.int32, (C, GROUP), 0)
    cw = lax.broadcasted_iota(jnp.int32, (C, GROUP), 1) % HEAD_DIM
    strict = rw > cw
    incl = rw >= cw
    eye = (rw == cw).astype(F32)

    def scores(u):
        lhs = jnp.concatenate([u["at"], u["rt"]], axis=0)
        sb = _wdot_nt(lhs, _bd(u["bt"], mask))
        sk = _wdot_nt(lhs, _bd(u["kt"], mask))
        u["p"] = jnp.where(strict, sb[0:C], 0.0)
        u["m_rb"] = jnp.where(incl, sb[C:2 * C], 0.0)
        u["l_ak"] = jnp.where(strict, sk[0:C], 0.0)
        u["m_rk"] = jnp.where(incl, sk[C:2 * C], 0.0)
        u["tinv"] = eye + u["p"]
        u["bt_b"] = _block_transpose(u["b_d"])
        u["bt_k"] = _block_transpose(u["k_d"])
        u["pcw"] = _block_transpose(u["pc"])

    def square(u):
        u["p"] = _wdot(u["p"], _bd(u["p"], mask))

    def double(u):
        both = _wdot(jnp.concatenate([u["p"], u["tinv"]], axis=0), _bd(u["p"], mask))
        u["p"] = both[0:C]
        u["tinv"] = u["tinv"] + both[C:2 * C]

    def double_last(u):
        u["tinv"] = u["tinv"] + _wdot(u["tinv"], _bd(u["p"], mask))

    def with_v(u):
        rv = _wdot(jnp.concatenate([u["l_ak"], u["m_rk"], u["bt_k"]], axis=0), _bd(u["v"], mask))
        u["x"], u["y0"], u["psi"] = rv[0:C], rv[C:2 * C], rv[2 * C:3 * C]

    def times_t(u):
        u["gt"] = _wdot(jnp.concatenate([u["bt_b"], u["m_rb"]], axis=0), _bd(u["tinv"], mask))

    def fold(u):
        rw_ = _wdot(u["gt"], _bd(u["at"], mask))
        ru = _wdot(u["gt"], _bd(u["x"], mask))
        u["po"] = jnp.concatenate([rw_[0:C], u["rt"] + rw_[C:2 * C]], axis=0)
        u["psi"] = u["psi"] + ru[0:C]
        u["y0"] = u["y0"] + ru[C:2 * C]

    hcur = list(hcur)
    ys = [[None] * n_groups for _ in range(n_chunks)]

    def carry(u):
        c, gi = u["c"], u["g"]
        ph = _wdot(u["po"], _bd(hcur[gi], mask))
        ys[c][gi] = ph[C:2 * C] + u["y0"]
        hcur[gi] = u["pcw"] * hcur[gi] + ph[0:C] + u["psi"]

    stages = [scores, square, double, double, double, double, double_last, with_v, times_t, fold, carry]

    units = {}
    for t in range(n_chunks + len(stages)):
        for c in range(n_chunks):
            st = t - c - 1
            if st == -1:
                fields, lpc = get(c)
                pc = jnp.broadcast_to(jnp.exp(lpc), (C, RWKV_WIDTH))
                units[c] = []
                for gi in range(n_groups):
                    ln = slice(gi * GROUP, (gi + 1) * GROUP)
                    u = {name: val[:, ln] for name, val in fields.items()}
                    u.update(c=c, g=gi, pc=pc[:, ln])
                    units[c].append(u)
            elif 0 <= st < len(stages):
                for u in units[c]:
                    stages[st](u)
        yield
    y = jnp.concatenate([jnp.concatenate(ys[c], axis=1) for c in range(n_chunks)], axis=0)
    return y, hcur


def _ret_mix(z_ref, cos_ref, sin_ref, dm_ref, qdec_ref, kdec_ref, gc_ref, scur, mask):
    C = CHUNK
    W = RET_WIDTH
    T = z_ref.shape[0]
    n_chunks = T // C
    n_groups = W // GROUP
    reps = W // cos_ref.shape[1]
    cosf = jnp.concatenate([cos_ref[...]] * reps, axis=1)
    sinf = jnp.concatenate([sin_ref[...]] * reps, axis=1)
    lane = lax.broadcasted_iota(jnp.int32, (1, W), 1)
    first_half = (lane % HEAD_DIM) < (HEAD_DIM // 2)

    def rot(t):
        swapped = jnp.where(first_half, pltpu.roll(t, W - HEAD_DIM // 2, 1), pltpu.roll(t, HEAD_DIM // 2, 1))
        return t * cosf + swapped * sinf

    q = rot(z_ref[:, 0:W]) * (HEAD_DIM ** -0.5)
    yield
    k = rot(z_ref[:, W:2 * W])
    v = z_ref[:, 2 * W:3 * W]
    yield

    units = []
    for c in range(n_chunks):
        rows = slice(c * C, (c + 1) * C)
        qd = q[rows] * qdec_ref[...]
        kd = k[rows] * kdec_ref[...]
        for gi in range(n_groups):
            ln = slice(gi * GROUP, (gi + 1) * GROUP)
            units.append(dict(c=c, g=gi, q=q[rows, ln], k=k[rows, ln], v=v[rows, ln], qd=qd[:, ln], kd=kd[:, ln]))
    for u in units:
        u["sc"] = _wdot_nt(u["q"], _bd(u["k"], mask)) * dm_ref[:, u["g"] * GROUP:(u["g"] + 1) * GROUP]
        u["kdt"] = _block_transpose(u["kd"])
    yield
    for u in units:
        both = _wdot(jnp.concatenate([u["sc"], u["kdt"]], axis=0), _bd(u["v"], mask))
        u["y"], u["kv"] = both[0:C], both[C:2 * C]
    yield

    scur = list(scur)
    ys = [[None] * n_groups for _ in range(n_chunks)]
    for u in units:
        c, gi = u["c"], u["g"]
        ys[c][gi] = u["y"] + _wdot(u["qd"], _bd(scur[gi], mask))
        scur[gi] = scur[gi] * gc_ref[:, gi * GROUP:(gi + 1) * GROUP] + u["kv"]
    y = jnp.concatenate([jnp.concatenate(ys[c], axis=1) for c in range(n_chunks)], axis=0)
    return y, scur


def _mixer_kernel(x_ref, g_ref, sc_ref, sh_ref, win_hbm,
                  mu_ref, w0_ref, w2_ref, a0_ref, a2_ref, g2_ref, kk_ref, ka_ref, rk_ref, lng_ref, lnb_ref,
                  cos_ref, sin_ref, dm_ref, qdec_ref, kdec_ref, gc_ref, gng_ref, mask_ref,
                  yr_ref, yt_ref, zr_ref, zt_ref, zlast_ref, rstate_ref, tstate_ref, win_ref, stg_ref, sem_ref):
    T = x_ref.shape[1]
    n_groups = RWKV_WIDTH // GROUP

    @pl.when(jnp.logical_and(pl.program_id(0) == 0, pl.program_id(1) == 0))
    def _():
        _cast_weight(win_hbm, win_ref, stg_ref, sem_ref, MXU_DIM)

    @pl.when(pl.program_id(1) == 0)
    def _():
        zlast_ref[...] = jnp.zeros_like(zlast_ref)
        rstate_ref[...] = jnp.zeros_like(rstate_ref)
        tstate_ref[...] = jnp.zeros_like(tstate_ref)

    h = (_rms(x_ref[0]) * g_ref[...] * (1.0 + sc_ref[0]) + sh_ref[0]).astype(BF16)
    step = INPROJ_COLS
    for j in range(0, RWKV_COLS, step):
        zr_ref[:, j:j + step] = jnp.dot(h, win_ref[:, j:j + step], preferred_element_type=F32)

    def ret_proj():
        for j in range(0, RET_COLS, step):
            zt_ref[:, j:j + step] = jnp.dot(h, win_ref[:, RWKV_COLS + j:RWKV_COLS + j + step],
                                            preferred_element_type=F32)
            yield

    mask = mask_ref[...]
    prm = dict(mu=mu_ref[...], w0=w0_ref[...], w2=w2_ref[...], a0=a0_ref[...], a2=a2_ref[...],
               g2=g2_ref[...], k_k=kk_ref[...], k_a=ka_ref[...], r_k=rk_ref[...])
    _, head = _interleave(ret_proj(), _rwkv_prep(zr_ref[...], zlast_ref[0:1, :], prm, mask))
    zlast_ref[0:1, :] = zr_ref[T - 1:T, :]

    (yt, scur), (y, hcur) = _interleave(
        _ret_mix(zt_ref, cos_ref, sin_ref, dm_ref, qdec_ref, kdec_ref, gc_ref,
                 [tstate_ref[gi] for gi in range(n_groups)], mask),
        _rwkv_solve(functools.partial(_rwkv_chunk_operands, head), [rstate_ref[gi] for gi in range(n_groups)],
                    mask, T // CHUNK))
    for gi in range(n_groups):
        rstate_ref[gi] = hcur[gi]
        tstate_ref[gi] = scur[gi]
    y = _head_norm(y, mask, RWKV_LN_EPS) * lng_ref[...] + lnb_ref[...]
    yr_ref[0] = ((y + head["bonus"]) * head["g"]).astype(BF16)
    gate = zt_ref[:, 3 * RET_WIDTH:4 * RET_WIDTH]
    yt_ref[0] = (gate * _sigmoid(gate) * (_head_norm(yt, mask, RET_GN_EPS) * gng_ref[...])).astype(BF16)


def _mixer(x, g, sc, sh, w_in, mu, w0, w2, a0, a2, g2, k_k, k_a, r_k, ln_g, ln_b,
           cos2, sin2, dmw, qdec, kdec, gcw, gn_g, mask):
    B, S, D = x.shape
    T = MIXER_TOKENS
    W = RWKV_WIDTH
    row = lambda b, t: (b, t, 0)
    per_b = lambda b, t: (b, 0, 0)
    vec = lambda a: a.reshape(1, -1)
    full = lambda a: pl.BlockSpec(a.shape, lambda b, t: (0, 0))
    tab = lambda a: pl.BlockSpec((T, a.shape[1]), lambda b, t: (t, 0))
    params = [vec(mu), vec(w0), w2.astype(BF16), vec(a0), a2.astype(BF16), g2.astype(BF16),
              vec(k_k), vec(k_a), vec(r_k), vec(ln_g), vec(ln_b)]
    tables = [dmw, qdec, kdec, gcw, vec(gn_g), mask]
    return pl.pallas_call(
        _mixer_kernel,
        grid=(B, S // T),
        in_specs=[pl.BlockSpec((1, T, D), row), full(vec(g)), pl.BlockSpec((1, 1, D), per_b),
                  pl.BlockSpec((1, 1, D), per_b), pl.BlockSpec(memory_space=pl.ANY)]
                 + [full(a) for a in params] + [tab(cos2), tab(sin2)] + [full(a) for a in tables],
        out_specs=[pl.BlockSpec((1, T, W), row), pl.BlockSpec((1, T, RET_WIDTH), row)],
        out_shape=[jax.ShapeDtypeStruct((B, S, W), BF16), jax.ShapeDtypeStruct((B, S, RET_WIDTH), BF16)],
        scratch_shapes=[pltpu.VMEM((T, RWKV_COLS), F32),
                        pltpu.VMEM((T, RET_COLS), F32),
                        pltpu.VMEM((SUBLANES, RWKV_COLS), F32),
                        pltpu.VMEM((W // GROUP, HEAD_DIM, GROUP), F32),
                        pltpu.VMEM((RET_WIDTH // GROUP, HEAD_DIM, GROUP), F32),
                        pltpu.VMEM(w_in.shape, BF16),
                        pltpu.VMEM((2, CAST_ROWS, CAST_COLS), F32),
                        pltpu.SemaphoreType.DMA((2,))],
        compiler_params=pltpu.CompilerParams(
            dimension_semantics=("arbitrary", "arbitrary"), vmem_limit_bytes=VMEM_LIMIT_BYTES),
        name="mixer",
    )(x, vec(g), sc, sh, w_in, *params, cos2, sin2, *tables)


def _ffn_kernel(x_ref, yr_ref, yt_ref, gta_ref, shf_ref, scf_ref, gtf_ref, wout_hbm, gnf_ref,
                wup_hbm, cw_ref, cb_ref, wdn_hbm, gfin_ref, o_ref, carry_ref, ubuf_ref, abuf_ref,
                wout_ref, wup_ref, wdn_ref, stg_ref, sem_ref):
    tm = x_ref.shape[1]

    @pl.when(jnp.logical_and(pl.program_id(0) == 0, pl.program_id(1) == 0))
    def _():
        _cast_weight(wout_hbm, wout_ref, stg_ref, sem_ref, CAST_COLS)
        _cast_weight(wup_hbm, wup_ref, stg_ref, sem_ref, CAST_COLS)
        _cast_weight(wdn_hbm, wdn_ref, stg_ref, sem_ref, CAST_COLS)

    @pl.when(pl.program_id(1) == 0)
    def _():
        carry_ref[...] = jnp.zeros_like(carry_ref)

    ymix = (jnp.dot(yr_ref[0], wout_ref[0:RWKV_WIDTH, :], preferred_element_type=F32)
            + jnp.dot(yt_ref[0], wout_ref[RWKV_WIDTH:, :], preferred_element_type=F32))
    x1 = x_ref[0] + gta_ref[0] * ymix
    q_rows = tm // SUBLANES

    def strided(t):
        return jnp.swapaxes(t.reshape(SUBLANES, q_rows, t.shape[1]), 0, 1).reshape(tm, t.shape[1])

    def unstrided(t):
        return jnp.swapaxes(t.reshape(q_rows, SUBLANES, t.shape[1]), 0, 1).reshape(tm, t.shape[1])

    h = strided(_rms(x1) * gnf_ref[...] * (1.0 + scf_ref[0]) + shf_ref[0]).astype(BF16)
    row8 = lax.broadcasted_iota(jnp.int32, (SUBLANES, 1), 0)

    def conv(u, col):
        tf = u.shape[1]
        prev = carry_ref[:, col:col + tf]
        last, before = u[tm - SUBLANES:tm], u[tm - 2 * SUBLANES:tm - SUBLANES]
        d1 = jnp.where(row8 == 0, prev[1:2, :], pltpu.roll(last, 1, 0))
        d2 = jnp.where(row8 == 0, prev[0:1, :], pltpu.roll(before, 1, 0))
        u1 = jnp.concatenate([d1, u[0:tm - SUBLANES]], axis=0)
        u2 = jnp.concatenate([d2, d1, u[0:tm - 2 * SUBLANES]], axis=0)
        carry_ref[0:1, col:col + tf] = before[SUBLANES - 1:SUBLANES, :]
        carry_ref[1:2, col:col + tf] = last[SUBLANES - 1:SUBLANES, :]
        cw = cw_ref[:, col:col + tf]
        return cb_ref[:, col:col + tf] + u2 * cw[0:1, :] + u1 * cw[1:2, :] + u * cw[2:3, :]

    tf = FFN_COLS
    tiles = list(range(0, D_FF, tf))

    def up(n):
        j, slot = tiles[n], n % 2
        ubuf_ref[slot, :, 0:tf] = jnp.dot(h, wup_ref[:, j:j + tf], preferred_element_type=F32)
        ubuf_ref[slot, :, tf:2 * tf] = jnp.dot(h, wup_ref[:, D_FF + j:D_FF + j + tf], preferred_element_type=F32)

    def glu(n):
        j, slot = tiles[n], n % 2
        val = conv(ubuf_ref[slot, :, 0:tf], j)
        gate = conv(ubuf_ref[slot, :, tf:2 * tf], D_FF + j)
        abuf_ref[:, j:j + tf] = (gate * _sigmoid(gate) * val).astype(BF16)

    up(0)
    for n in range(1, len(tiles)):
        up(n)
        glu(n - 1)
    glu(len(tiles) - 1)
    y = unstrided(jnp.dot(abuf_ref[...], wdn_ref[...], preferred_element_type=F32))
    x2 = x1 + gtf_ref[0] * y
    o_ref[0] = _rms(x2) * gfin_ref[...]


def _ffn(x, y_rwkv, y_ret, gt_a, sh_f, sc_f, gt_f, w_out, g_ffn, w_up, conv_w, conv_b, w_down, g_fin):
    B, S, D = x.shape
    tm = FFN_ROWS
    row = lambda b, t: (b, t, 0)
    per_b = lambda b, t: (b, 0, 0)
    const = lambda b, t: (0, 0)
    once = lambda a: pl.BlockSpec(a.shape, const, pipeline_mode=pl.Buffered(1))
    hbm = pl.BlockSpec(memory_space=pl.ANY)
    vecs = [g_ffn.reshape(1, D)]
    return pl.pallas_call(
        _ffn_kernel,
        grid=(B, S // tm),
        in_specs=[pl.BlockSpec((1, tm, D), row),
                  pl.BlockSpec((1, tm, RWKV_WIDTH), row),
                  pl.BlockSpec((1, tm, RET_WIDTH), row),
                  pl.BlockSpec((1, 1, D), per_b), pl.BlockSpec((1, 1, D), per_b),
                  pl.BlockSpec((1, 1, D), per_b), pl.BlockSpec((1, 1, D), per_b),
                  hbm, once(vecs[0]), hbm, once(conv_w), pl.BlockSpec((1, 2 * D_FF), const),
                  hbm, pl.BlockSpec((1, D), const)],
        out_specs=pl.BlockSpec((1, tm, D), row),
        out_shape=jax.ShapeDtypeStruct((B, S, D), F32),
        scratch_shapes=[pltpu.VMEM((SUBLANES, 2 * D_FF), F32),
                        pltpu.VMEM((2, tm, 2 * FFN_COLS), F32),
                        pltpu.VMEM((tm, D_FF), BF16),
                        pltpu.VMEM(w_out.shape, BF16), pltpu.VMEM(w_up.shape, BF16), pltpu.VMEM(w_down.shape, BF16),
                        pltpu.VMEM((2, CAST_ROWS, CAST_COLS), F32),
                        pltpu.SemaphoreType.DMA((2,))],
        compiler_params=pltpu.CompilerParams(
            dimension_semantics=("arbitrary", "arbitrary"), vmem_limit_bytes=VMEM_LIMIT_BYTES),
        name="ffn",
    )(x, y_rwkv, y_ret, gt_a, sh_f, sc_f, gt_f, w_out, vecs[0], w_up, conv_w,
      conv_b.reshape(1, 2 * D_FF), w_down, g_fin.reshape(1, D))


def _tables(S):
    pos = np.arange(S, dtype=np.float64)
    inv_freq = ROPE_BASE ** (-np.arange(0, HEAD_DIM, 2, dtype=np.float64) / HEAD_DIM)
    ang = pos[:, None] * inv_freq[None, :]
    cos, sin = np.cos(ang), np.sin(ang)
    cos2 = np.concatenate([cos, cos, cos, cos], axis=1)
    sin2 = np.concatenate([-sin, sin, -sin, sin], axis=1)
    log_gamma = np.log1p(-(2.0 ** (-5.0 - np.arange(N_HEADS, dtype=np.float64))))
    idx = np.arange(CHUNK, dtype=np.float64)
    dmat = np.exp(log_gamma[:, None, None] * np.abs(idx[:, None] - idx[None, :]))
    q_dec = np.exp(log_gamma[:, None] * (idx + 1.0))
    k_dec = np.exp(log_gamma[:, None] * (CHUNK - 1.0 - idx))
    widen = lambda t: np.repeat(t.T, HEAD_DIM, axis=1)
    dmw = dmat.transpose(1, 0, 2).reshape(CHUNK, N_HEADS * CHUNK)
    gcw = np.repeat(np.exp(log_gamma * CHUNK), HEAD_DIM)[None, :]
    head = np.arange(GROUP) // HEAD_DIM
    mask = head[:, None] == head[None, :]
    f32 = lambda t: jnp.asarray(t, F32)
    return f32(cos2), f32(sin2), f32(dmw), f32(widen(q_dec)), f32(widen(k_dec)), f32(gcw), jnp.asarray(mask, BF16)


def kernel(x, c, w_ada, b_ada, attn_norm_g, w_in, rwkv_mu, rwkv_w0, rwkv_w2, rwkv_a0, rwkv_a2, rwkv_g2,
           rwkv_k_k, rwkv_k_a, rwkv_r_k, rwkv_ln_g, rwkv_ln_b, ret_gn_g, w_out, ffn_norm_g, ffn_w_up,
           ffn_conv_w, ffn_conv_b, ffn_w_down, final_norm_g):
    B, S, D = x.shape
    assert D == D_MODEL and S % MIXER_TOKENS == 0 and S % FFN_ROWS == 0 and w_ada.shape[0] == 1
    cos2, sin2, dmw, qdec, kdec, gcw, mask = _tables(S)
    mod = _ada(c, w_ada[0], b_ada[0])
    sh_a, sc_a, gt_a, sh_f, sc_f, gt_f = [m.reshape(B, 1, D) for m in jnp.split(mod, N_MOD, axis=-1)]
    y_rwkv, y_ret = _mixer(x, attn_norm_g[0], sc_a, sh_a, w_in.reshape(w_in.shape[1:]),
                           rwkv_mu[0], rwkv_w0[0], rwkv_w2[0], rwkv_a0[0], rwkv_a2[0], rwkv_g2[0],
                           rwkv_k_k[0], rwkv_k_a[0], rwkv_r_k[0], rwkv_ln_g[0], rwkv_ln_b[0],
                           cos2, sin2, dmw, qdec, kdec, gcw, ret_gn_g[0], mask)
    return _ffn(x, y_rwkv, y_ret, gt_a, sh_f, sc_f, gt_f, w_out.reshape(w_out.shape[1:]), ffn_norm_g[0],
                ffn_w_up.reshape(ffn_w_up.shape[1:]), ffn_conv_w[0], ffn_conv_b[0],
                ffn_w_down.reshape(ffn_w_down.shape[1:]), final_norm_g)
```

```python
import functools
import math

import jax
import jax.numpy as jnp
import numpy as np
from jax import lax
from jax.experimental import pallas as pl
from jax.experimental.pallas import tpu as pltpu

F32 = jnp.float32
BF16 = jnp.bfloat16

D_MODEL = 1024
CHUNK = 64
HEAD_DIM = 64
RWKV_WIDTH = 512
RET_WIDTH = 512
N_HEADS = 8
DECAY_LORA = 64
AAA_LORA = 64
GATE_LORA = 128
RWKV_COLS = 3 * RWKV_WIDTH + DECAY_LORA + AAA_LORA + GATE_LORA
RET_COLS = 4 * RET_WIDTH
D_FF = 2816
ROPE_BASE = 10000.0
NORM_EPS = 1e-6
RWKV_LN_EPS = 64e-5
RET_GN_EPS = 1e-6
W_DECAY_SCALE = math.exp(-0.5)
N_MOD = 6
MXU_DIM = 256
SUBLANES = 8
GROUP = MXU_DIM
HEADS_PER_GROUP = GROUP // HEAD_DIM
KK_NORM_FLOOR = 1e-12

VMEM_LIMIT_BYTES = 56 * 1024 * 1024
MIXER_TOKENS = 512
FFN_ROWS = 512
FFN_COLS = MXU_DIM
INPROJ_COLS = MXU_DIM
ADA_COLS = 1536

_NT = (((1,), (1,)), ((), ()))


def _bdot(a, b):
    return jnp.dot(a.astype(BF16), b.astype(BF16), preferred_element_type=F32)


def _split_parts(x, parts=3):
    out, rem = [], x
    for i in range(parts):
        p = rem.astype(BF16)
        out.append(p)
        if i + 1 < parts:
            rem = rem - p.astype(F32)
    return out


def _head_sums(x, mask, parts=2):
    outs = []
    for g in range(x.shape[1] // GROUP):
        acc = None
        for p in _split_parts(x[:, g * GROUP:(g + 1) * GROUP], parts=parts):
            d = jnp.dot(p, mask, preferred_element_type=F32)
            acc = d if acc is None else acc + d
        outs.append(acc)
    return jnp.concatenate(outs, axis=1)


def _exact_dot_left(m_bf16, x):
    acc = None
    for p in _split_parts(x):
        d = jnp.dot(m_bf16, p, preferred_element_type=F32)
        acc = d if acc is None else acc + d
    return acc


def _sigmoid(x):
    return 1.0 / (1.0 + jnp.exp(-x))


def _rms(x):
    return x * lax.rsqrt(jnp.mean(x * x, axis=-1, keepdims=True) + NORM_EPS)


def _ada_kernel(c_ref, w_ref, b_ref, o_ref):
    cv = c_ref[...]
    s = cv * _sigmoid(cv)
    o_ref[...] = jnp.dot(s, w_ref[...], preferred_element_type=F32,
                         precision=lax.Precision.HIGHEST) + b_ref[...]


def _ada(c, w, b):
    B = c.shape[0]
    n = w.shape[1]
    return pl.pallas_call(
        _ada_kernel,
        grid=(n // ADA_COLS,),
        in_specs=[pl.BlockSpec((B, D_MODEL), lambda j: (0, 0)),
                  pl.BlockSpec((D_MODEL, ADA_COLS), lambda j: (0, j)),
                  pl.BlockSpec((1, ADA_COLS), lambda j: (0, j))],
        out_specs=pl.BlockSpec((B, ADA_COLS), lambda j: (0, j)),
        out_shape=jax.ShapeDtypeStruct((B, n), F32),
        compiler_params=pltpu.CompilerParams(vmem_limit_bytes=VMEM_LIMIT_BYTES),
        name="ada",
    )(c, w, b.reshape(1, n))


def _head_norm(y, mask, eps):
    mean = _head_sums(y, mask) * (1.0 / HEAD_DIM)
    d = y - mean
    var = _head_sums(d * d, mask, parts=1) * (1.0 / HEAD_DIM)
    return d * lax.rsqrt(var + eps)


def _shift_rows(x, first):
    rolled = pltpu.roll(x, 1, 0)
    head = jnp.where(lax.broadcasted_iota(jnp.int32, (SUBLANES, 1), 0) == 0, first, rolled[0:SUBLANES])
    return jnp.concatenate([head, rolled[SUBLANES:]], axis=0)


def _bd(x, mask):
    xb = x.astype(BF16)
    half = GROUP // 2
    zeros = jnp.zeros((HEAD_DIM, half), BF16)
    blocks = []
    for h in range(HEADS_PER_GROUP):
        rows = slice(h * HEAD_DIM, (h + 1) * HEAD_DIM)
        if h < HEADS_PER_GROUP // 2:
            blocks.append(jnp.concatenate([xb[:, 0:half] * mask[rows, 0:half], zeros], axis=1))
        else:
            blocks.append(jnp.concatenate([zeros, xb[:, half:GROUP] * mask[rows, half:GROUP]], axis=1))
    return jnp.concatenate(blocks, axis=0)


def _block_transpose(x):
    r = jnp.concatenate([x] * HEADS_PER_GROUP, axis=0).T
    blk = lax.broadcasted_iota(jnp.int32, (1, GROUP), 1) // HEAD_DIM
    out = r[0:HEAD_DIM]
    for h in range(1, HEADS_PER_GROUP):
        out = jnp.where(blk == h, r[h * HEAD_DIM:(h + 1) * HEAD_DIM], out)
    return out


def _wdot(lhs, rhs_bd):
    return jnp.dot(lhs.astype(BF16), rhs_bd, preferred_element_type=F32)


def _wdot_nt(lhs, rhs_bd):
    return lax.dot_general(lhs.astype(BF16), rhs_bd, _NT, preferred_element_type=F32)


def _interleave(*gens):
    results = [None] * len(gens)
    live = list(range(len(gens)))
    while live:
        for i in list(live):
            try:
                next(gens[i])
            except StopIteration as stop:
                results[i] = stop.value
                live.remove(i)
    return results


def _rwkv_prep(z, zfirst, prm, mask):
    W = RWKV_WIDTH
    zprev = _shift_rows(z, zfirst)
    zm = z + prm["mu"] * (zprev - z)
    r = zm[:, 0:W]
    k = zm[:, W:2 * W]
    v = zm[:, 2 * W:3 * W]
    wd = zm[:, 3 * W:3 * W + DECAY_LORA]
    ad = zm[:, 3 * W + DECAY_LORA:3 * W + DECAY_LORA + AAA_LORA]
    gd = zm[:, 3 * W + DECAY_LORA + AAA_LORA:RWKV_COLS]

    lw = -W_DECAY_SCALE * _sigmoid(prm["w0"] + _bdot(jnp.tanh(wd), prm["w2"]))
    a = _sigmoid(prm["a0"] + _bdot(ad, prm["a2"]))
    g = _bdot(_sigmoid(gd), prm["g2"])
    kkr = k * prm["k_k"]
    yield
    kkn = kkr * lax.rsqrt(jnp.maximum(_head_sums(kkr * kkr, mask, parts=1), KK_NORM_FLOOR ** 2))
    km = k * (1.0 + (a - 1.0) * prm["k_a"])
    yield
    bonus = _head_sums(r * km * prm["r_k"], mask) * v
    return dict(r=r, v=v, lw=lw, kkn=kkn, km=km, kb=kkn * a, bonus=bonus, g=g)


def _rwkv_chunk_operands(head, c):
    C = CHUNK
    rows = slice(c * C, (c + 1) * C)
    ri = lax.broadcasted_iota(jnp.int32, (C, C), 0)
    ci = lax.broadcasted_iota(jnp.int32, (C, C), 1)
    lw_c = head["lw"][rows]
    lp = _exact_dot_left((ri >= ci).astype(BF16), lw_c)
    em = jnp.exp(-lp)
    lpc = lp[C - 1:C, :]
    ee = jnp.exp(lpc - lp)
    kb, km = head["kb"][rows], head["km"][rows]
    fields = dict(rt=head["r"][rows] * jnp.exp(lp), at=-head["kkn"][rows] * jnp.exp(lp - lw_c),
                  bt=kb * em, kt=km * em, b_d=kb * ee, k_d=km * ee, v=head["v"][rows])
    return fields, lpc


def _rwkv_solve(get, hcur, mask, n_chunks):
    C = CHUNK
    n_groups = RWKV_WIDTH // GROUP
    rw = lax.broadcasted_iota(jnp.int32, (C, GROUP), 0)
    cw = lax.broadcasted_iota(jnp.int32, (C, GROUP), 1) % HEAD_DIM
    strict = rw > cw
    incl = rw >= cw
    eye = (rw == cw).astype(F32)

    def scores(u):
        lhs = jnp.concatenate([u["at"], u["rt"]], axis=0)
        sb = _wdot_nt(lhs, _bd(u["bt"], mask))
        sk = _wdot_nt(lhs, _bd(u["kt"], mask))
        u["p"] = jnp.where(strict, sb[0:C], 0.0)
        u["m_rb"] = jnp.where(incl, sb[C:2 * C], 0.0)
        u["l_ak"] = jnp.where(strict, sk[0:C], 0.0)
        u["m_rk"] = jnp.where(incl, sk[C:2 * C], 0.0)
        u["tinv"] = eye + u["p"]
        u["bt_b"] = _block_transpose(u["b_d"])
        u["bt_k"] = _block_transpose(u["k_d"])
        u["pcw"] = _block_transpose(u["pc"])

    def square(u):
        u["p"] = _wdot(u["p"], _bd(u["p"], mask))

    def double(u):
        both = _wdot(jnp.concatenate([u["p"], u["tinv"]], axis=0), _bd(u["p"], mask))
        u["p"] = both[0:C]
        u["tinv"] = u["tinv"] + both[C:2 * C]

    def double_last(u):
        u["tinv"] = u["tinv"] + _wdot(u["tinv"], _bd(u["p"], mask))

    def with_v(u):
        rv = _wdot(jnp.concatenate([u["l_ak"], u["m_rk"], u["bt_k"]], axis=0), _bd(u["v"], mask))
        u["x"], u["y0"], u["psi"] = rv[0:C], rv[C:2 * C], rv[2 * C:3 * C]

    def times_t(u):
        u["gt"] = _wdot(jnp.concatenate([u["bt_b"], u["m_rb"]], axis=0), _bd(u["tinv"], mask))

    def fold(u):
        rw_ = _wdot(u["gt"], _bd(u["at"], mask))
        ru = _wdot(u["gt"], _bd(u["x"], mask))
        u["po"] = jnp.concatenate([rw_[0:C], u["rt"] + rw_[C:2 * C]], axis=0)
        u["psi"] = u["psi"] + ru[0:C]
        u["y0"] = u["y0"] + ru[C:2 * C]

    hcur = list(hcur)
    ys = [[None] * n_groups for _ in range(n_chunks)]

    def carry(u):
        c, gi = u["c"], u["g"]
        ph = _wdot(u["po"], _bd(hcur[gi], mask))
        ys[c][gi] = ph[C:2 * C] + u["y0"]
        hcur[gi] = u["pcw"] * hcur[gi] + ph[0:C] + u["psi"]

    stages = [scores, square, double, double, double, double, double_last, with_v, times_t, fold, carry]

    units = {}
    for t in range(n_chunks + len(stages)):
        for c in range(n_chunks):
            st = t - c - 1
            if st == -1:
                fields, lpc = get(c)
                pc = jnp.broadcast_to(jnp.exp(lpc), (C, RWKV_WIDTH))
                units[c] = []
                for gi in range(n_groups):
                    ln = slice(gi * GROUP, (gi + 1) * GROUP)
                    u = {name: val[:, ln] for name, val in fields.items()}
                    u.update(c=c, g=gi, pc=pc[:, ln])
                    units[c].append(u)
            elif 0 <= st < len(stages):
                for u in units[c]:
                    stages[st](u)
        yield
    y = jnp.concatenate([jnp.concatenate(ys[c], axis=1) for c in range(n_chunks)], axis=0)
    return y, hcur


def _ret_mix(z_ref, cos_ref, sin_ref, dm_ref, qdec_ref, kdec_ref, gc_ref, scur, mask):
    C = CHUNK
    W = RET_WIDTH
    T = z_ref.shape[0]
    n_chunks = T // C
    n_groups = W // GROUP
    reps = W // cos_ref.shape[1]
    cosf = jnp.concatenate([cos_ref[...]] * reps, axis=1)
    sinf = jnp.concatenate([sin_ref[...]] * reps, axis=1)
    lane = lax.broadcasted_iota(jnp.int32, (1, W), 1)
    first_half = (lane % HEAD_DIM) < (HEAD_DIM // 2)

    def rot(t):
        swapped = jnp.where(first_half, pltpu.roll(t, W - HEAD_DIM // 2, 1), pltpu.roll(t, HEAD_DIM // 2, 1))
        return t * cosf + swapped * sinf

    q = rot(z_ref[:, 0:W]) * (HEAD_DIM ** -0.5)
    yield
    k = rot(z_ref[:, W:2 * W])
    v = z_ref[:, 2 * W:3 * W]
    yield

    units = []
    for c in range(n_chunks):
        rows = slice(c * C, (c + 1) * C)
        qd = q[rows] * qdec_ref[...]
        kd = k[rows] * kdec_ref[...]
        for gi in range(n_groups):
            ln = slice(gi * GROUP, (gi + 1) * GROUP)
            units.append(dict(c=c, g=gi, q=q[rows, ln], k=k[rows, ln], v=v[rows, ln], qd=qd[:, ln], kd=kd[:, ln]))
    for u in units:
        u["kdt"] = _block_transpose(u["kd"])
        u["sc"] = _wdot(u["q"], _bd(u["kdt"], mask)) * dm_ref[:, u["g"] * GROUP:(u["g"] + 1) * GROUP]
    yield
    for u in units:
        both = _wdot(jnp.concatenate([u["sc"], u["kdt"]], axis=0), _bd(u["v"], mask))
        u["y"], u["kv"] = both[0:C], both[C:2 * C]
    yield

    scur = list(scur)
    ys = [[None] * n_groups for _ in range(n_chunks)]
    for u in units:
        c, gi = u["c"], u["g"]
        ys[c][gi] = u["y"] + _wdot(u["qd"], _bd(scur[gi], mask))
        scur[gi] = scur[gi] * gc_ref[:, gi * GROUP:(gi + 1) * GROUP] + u["kv"]
    y = jnp.concatenate([jnp.concatenate(ys[c], axis=1) for c in range(n_chunks)], axis=0)
    return y, scur


def _mixer_kernel(x_ref, g_ref, sc_ref, sh_ref, win_ref,
                  mu_ref, w0_ref, w2_ref, a0_ref, a2_ref, g2_ref, kk_ref, ka_ref, rk_ref, lng_ref, lnb_ref,
                  cos_ref, sin_ref, dm_ref, qdec_ref, kdec_ref, gc_ref, gng_ref, mask_ref,
                  yr_ref, yt_ref, zr_ref, zt_ref, zlast_ref, rstate_ref, tstate_ref):
    T = x_ref.shape[1]
    n_groups = RWKV_WIDTH // GROUP

    @pl.when(pl.program_id(1) == 0)
    def _():
        zlast_ref[...] = jnp.zeros_like(zlast_ref)
        rstate_ref[...] = jnp.zeros_like(rstate_ref)
        tstate_ref[...] = jnp.zeros_like(tstate_ref)

    h = (_rms(x_ref[0]) * g_ref[...] * (1.0 + sc_ref[0]) + sh_ref[0]).astype(BF16)
    step = INPROJ_COLS
    for j in range(0, RWKV_COLS, step):
        zr_ref[:, j:j + step] = jnp.dot(h, win_ref[:, j:j + step], preferred_element_type=F32)

    def ret_proj():
        for j in range(0, RET_COLS, step):
            zt_ref[:, j:j + step] = jnp.dot(h, win_ref[:, RWKV_COLS + j:RWKV_COLS + j + step],
                                            preferred_element_type=F32)
            yield

    mask = mask_ref[...]
    prm = dict(mu=mu_ref[...], w0=w0_ref[...], w2=w2_ref[...], a0=a0_ref[...], a2=a2_ref[...],
               g2=g2_ref[...], k_k=kk_ref[...], k_a=ka_ref[...], r_k=rk_ref[...])
    _, head = _interleave(ret_proj(), _rwkv_prep(zr_ref[...], zlast_ref[0:1, :], prm, mask))
    zlast_ref[0:1, :] = zr_ref[T - 1:T, :]

    (yt, scur), (y, hcur) = _interleave(
        _ret_mix(zt_ref, cos_ref, sin_ref, dm_ref, qdec_ref, kdec_ref, gc_ref,
                 [tstate_ref[gi] for gi in range(n_groups)], mask),
        _rwkv_solve(functools.partial(_rwkv_chunk_operands, head), [rstate_ref[gi] for gi in range(n_groups)],
                    mask, T // CHUNK))
    for gi in range(n_groups):
        rstate_ref[gi] = hcur[gi]
        tstate_ref[gi] = scur[gi]
    y = _head_norm(y, mask, RWKV_LN_EPS) * lng_ref[...] + lnb_ref[...]
    yr_ref[0] = ((y + head["bonus"]) * head["g"]).astype(BF16)
    gate = zt_ref[:, 3 * RET_WIDTH:4 * RET_WIDTH]
    yt_ref[0] = (gate * _sigmoid(gate) * (_head_norm(yt, mask, RET_GN_EPS) * gng_ref[...])).astype(BF16)


def _mixer(x, g, sc, sh, w_in, mu, w0, w2, a0, a2, g2, k_k, k_a, r_k, ln_g, ln_b,
           cos2, sin2, dmw, qdec, kdec, gcw, gn_g, mask):
    B, S, D = x.shape
    T = MIXER_TOKENS
    W = RWKV_WIDTH
    row = lambda b, t: (b, t, 0)
    per_b = lambda b, t: (b, 0, 0)
    vec = lambda a: a.reshape(1, -1)
    full = lambda a: pl.BlockSpec(a.shape, lambda b, t: (0, 0))
    tab = lambda a: pl.BlockSpec((T, a.shape[1]), lambda b, t: (t, 0))
    params = [vec(mu), vec(w0), w2.astype(BF16), vec(a0), a2.astype(BF16), g2.astype(BF16),
              vec(k_k), vec(k_a), vec(r_k), vec(ln_g), vec(ln_b)]
    tables = [dmw, qdec, kdec, gcw, vec(gn_g), mask]
    return pl.pallas_call(
        _mixer_kernel,
        grid=(B, S // T),
        in_specs=[pl.BlockSpec((1, T, D), row), full(vec(g)), pl.BlockSpec((1, 1, D), per_b),
                  pl.BlockSpec((1, 1, D), per_b), full(w_in)]
                 + [full(a) for a in params] + [tab(cos2), tab(sin2)] + [full(a) for a in tables],
        out_specs=[pl.BlockSpec((1, T, W), row), pl.BlockSpec((1, T, RET_WIDTH), row)],
        out_shape=[jax.ShapeDtypeStruct((B, S, W), BF16), jax.ShapeDtypeStruct((B, S, RET_WIDTH), BF16)],
        scratch_shapes=[pltpu.VMEM((T, RWKV_COLS), F32),
                        pltpu.VMEM((T, RET_COLS), F32),
                        pltpu.VMEM((SUBLANES, RWKV_COLS), F32),
                        pltpu.VMEM((W // GROUP, HEAD_DIM, GROUP), F32),
                        pltpu.VMEM((RET_WIDTH // GROUP, HEAD_DIM, GROUP), F32)],
        compiler_params=pltpu.CompilerParams(
            dimension_semantics=("parallel", "arbitrary"), vmem_limit_bytes=VMEM_LIMIT_BYTES),
        name="mixer",
    )(x, vec(g), sc, sh, w_in, *params, cos2, sin2, *tables)


def _ffn_kernel(x_ref, yr_ref, yt_ref, gta_ref, shf_ref, scf_ref, gtf_ref, wout_ref, gnf_ref,
                wup_ref, cw_ref, cb_ref, wdn_ref, gfin_ref, o_ref, carry_ref, ubuf_ref, abuf_ref):
    tm = x_ref.shape[1]

    @pl.when(pl.program_id(1) == 0)
    def _():
        carry_ref[...] = jnp.zeros_like(carry_ref)

    ymix = (jnp.dot(yr_ref[0], wout_ref[0:RWKV_WIDTH, :], preferred_element_type=F32)
            + jnp.dot(yt_ref[0], wout_ref[RWKV_WIDTH:, :], preferred_element_type=F32))
    x1 = x_ref[0] + gta_ref[0] * ymix
    q_rows = tm // SUBLANES

    def strided(t):
        return jnp.swapaxes(t.reshape(SUBLANES, q_rows, t.shape[1]), 0, 1).reshape(tm, t.shape[1])

    def unstrided(t):
        return jnp.swapaxes(t.reshape(q_rows, SUBLANES, t.shape[1]), 0, 1).reshape(tm, t.shape[1])

    h = strided(_rms(x1) * gnf_ref[...] * (1.0 + scf_ref[0]) + shf_ref[0]).astype(BF16)
    row8 = lax.broadcasted_iota(jnp.int32, (SUBLANES, 1), 0)

    def conv(u, col):
        tf = u.shape[1]
        prev = carry_ref[:, col:col + tf]
        last, before = u[tm - SUBLANES:tm], u[tm - 2 * SUBLANES:tm - SUBLANES]
        d1 = jnp.where(row8 == 0, prev[1:2, :], pltpu.roll(last, 1, 0))
        d2 = jnp.where(row8 == 0, prev[0:1, :], pltpu.roll(before, 1, 0))
        u1 = jnp.concatenate([d1, u[0:tm - SUBLANES]], axis=0)
        u2 = jnp.concatenate([d2, d1, u[0:tm - 2 * SUBLANES]], axis=0)
        carry_ref[0:1, col:col + tf] = before[SUBLANES - 1:SUBLANES, :]
        carry_ref[1:2, col:col + tf] = last[SUBLANES - 1:SUBLANES, :]
        cw = cw_ref[:, col:col + tf]
        return cb_ref[:, col:col + tf] + u2 * cw[0:1, :] + u1 * cw[1:2, :] + u * cw[2:3, :]

    tf = FFN_COLS
    tiles = list(range(0, D_FF, tf))

    def up(n):
        j, slot = tiles[n], n % 2
        ubuf_ref[slot, :, 0:tf] = jnp.dot(h, wup_ref[:, j:j + tf], preferred_element_type=F32)
        ubuf_ref[slot, :, tf:2 * tf] = jnp.dot(h, wup_ref[:, D_FF + j:D_FF + j + tf], preferred_element_type=F32)

    def glu(n):
        j, slot = tiles[n], n % 2
        val = conv(ubuf_ref[slot, :, 0:tf], j)
        gate = conv(ubuf_ref[slot, :, tf:2 * tf], D_FF + j)
        abuf_ref[:, j:j + tf] = (gate * _sigmoid(gate) * val).astype(BF16)

    up(0)
    for n in range(1, len(tiles)):
        up(n)
        glu(n - 1)
    glu(len(tiles) - 1)
    y = unstrided(jnp.dot(abuf_ref[...], wdn_ref[...], preferred_element_type=F32))
    x2 = x1 + gtf_ref[0] * y
    o_ref[0] = _rms(x2) * gfin_ref[...]


def _ffn(x, y_rwkv, y_ret, gt_a, sh_f, sc_f, gt_f, w_out, g_ffn, w_up, conv_w, conv_b, w_down, g_fin):
    B, S, D = x.shape
    tm = FFN_ROWS
    row = lambda b, t: (b, t, 0)
    per_b = lambda b, t: (b, 0, 0)
    const = lambda b, t: (0, 0)
    once = lambda a: pl.BlockSpec(a.shape, const, pipeline_mode=pl.Buffered(1))
    vecs = [g_ffn.reshape(1, D)]
    return pl.pallas_call(
        _ffn_kernel,
        grid=(B, S // tm),
        in_specs=[pl.BlockSpec((1, tm, D), row),
                  pl.BlockSpec((1, tm, RWKV_WIDTH), row),
                  pl.BlockSpec((1, tm, RET_WIDTH), row),
                  pl.BlockSpec((1, 1, D), per_b), pl.BlockSpec((1, 1, D), per_b),
                  pl.BlockSpec((1, 1, D), per_b), pl.BlockSpec((1, 1, D), per_b),
                  once(w_out), once(vecs[0]), once(w_up), once(conv_w), pl.BlockSpec((1, 2 * D_FF), const),
                  once(w_down), pl.BlockSpec((1, D), const)],
        out_specs=pl.BlockSpec((1, tm, D), row),
        out_shape=jax.ShapeDtypeStruct((B, S, D), F32),
        scratch_shapes=[pltpu.VMEM((SUBLANES, 2 * D_FF), F32),
                        pltpu.VMEM((2, tm, 2 * FFN_COLS), F32),
                        pltpu.VMEM((tm, D_FF), BF16)],
        compiler_params=pltpu.CompilerParams(
            dimension_semantics=("parallel", "arbitrary"), vmem_limit_bytes=VMEM_LIMIT_BYTES),
        name="ffn",
    )(x, y_rwkv, y_ret, gt_a, sh_f, sc_f, gt_f, w_out, vecs[0], w_up, conv_w,
      conv_b.reshape(1, 2 * D_FF), w_down, g_fin.reshape(1, D))


def _tables(S):
    pos = np.arange(S, dtype=np.float64)
    inv_freq = ROPE_BASE ** (-np.arange(0, HEAD_DIM, 2, dtype=np.float64) / HEAD_DIM)
    ang = pos[:, None] * inv_freq[None, :]
    cos, sin = np.cos(ang), np.sin(ang)
    cos2 = np.concatenate([cos, cos, cos, cos], axis=1)
    sin2 = np.concatenate([-sin, sin, -sin, sin], axis=1)
    log_gamma = np.log1p(-(2.0 ** (-5.0 - np.arange(N_HEADS, dtype=np.float64))))
    idx = np.arange(CHUNK, dtype=np.float64)
    dmat = np.exp(log_gamma[:, None, None] * np.abs(idx[:, None] - idx[None, :]))
    q_dec = np.exp(log_gamma[:, None] * (idx + 1.0))
    k_dec = np.exp(log_gamma[:, None] * (CHUNK - 1.0 - idx))
    widen = lambda t: np.repeat(t.T, HEAD_DIM, axis=1)
    dmw = (dmat / k_dec[:, None, :]).transpose(1, 0, 2).reshape(CHUNK, N_HEADS * CHUNK)
    gcw = np.repeat(np.exp(log_gamma * CHUNK), HEAD_DIM)[None, :]
    head = np.arange(GROUP) // HEAD_DIM
    mask = head[:, None] == head[None, :]
    f32 = lambda t: jnp.asarray(t, F32)
    return f32(cos2), f32(sin2), f32(dmw), f32(widen(q_dec)), f32(widen(k_dec)), f32(gcw), jnp.asarray(mask, BF16)


def kernel(x, c, w_ada, b_ada, attn_norm_g, w_in, rwkv_mu, rwkv_w0, rwkv_w2, rwkv_a0, rwkv_a2, rwkv_g2,
           rwkv_k_k, rwkv_k_a, rwkv_r_k, rwkv_ln_g, rwkv_ln_b, ret_gn_g, w_out, ffn_norm_g, ffn_w_up,
           ffn_conv_w, ffn_conv_b, ffn_w_down, final_norm_g):
    B, S, D = x.shape
    assert D == D_MODEL and S % MIXER_TOKENS == 0 and S % FFN_ROWS == 0 and w_ada.shape[0] == 1
    cos2, sin2, dmw, qdec, kdec, gcw, mask = _tables(S)
    mod = _ada(c, w_ada[0], b_ada[0])
    sh_a, sc_a, gt_a, sh_f, sc_f, gt_f = [m.reshape(B, 1, D) for m in jnp.split(mod, N_MOD, axis=-1)]
    y_rwkv, y_ret = _mixer(x, attn_norm_g[0], sc_a, sh_a, w_in[0].astype(BF16),
                           rwkv_mu[0], rwkv_w0[0], rwkv_w2[0], rwkv_a0[0], rwkv_a2[0], rwkv_g2[0],
                           rwkv_k_k[0], rwkv_k_a[0], rwkv_r_k[0], rwkv_ln_g[0], rwkv_ln_b[0],
                           cos2, sin2, dmw, qdec, kdec, gcw, ret_gn_g[0], mask)
    return _ffn(x, y_rwkv, y_ret, gt_a, sh_f, sc_f, gt_f, w_out[0].astype(BF16), ffn_norm_g[0],
                ffn_w_up[0].astype(BF16), ffn_conv_w[0], ffn_conv_b[0], ffn_w_down[0].astype(BF16),
                final_norm_g)
```
